```python
import jax, jax.numpy as jnp
from jax import lax
import numpy as np

D_MODEL = 1024
BATCH = 2
SEQ = 8192
DEPTH = 2
DEC_BATCH = 32
DEC_SEQ = 4
PAST_LEN = 16384
PAGE_SIZE = 128

N_META = 16
N_A_LAYERS = DEPTH // 2
N_B_LAYERS = DEPTH - N_A_LAYERS
POOL_WINDOWS = (2, 4, 8, 16)
N_POOL_GROUPS = len(POOL_WINDOWS)
POOL_GROUP = D_MODEL // N_POOL_GROUPS
POOL_CTX = max(POOL_WINDOWS) - 1
HEAD_DIM = 64
N_HEADS = D_MODEL // HEAD_DIM
D_FF = 4 * D_MODEL
Q_BLOCK = 128
LN_EPS = 1e-5
DEEPNORM_ALPHA = (2.0 * DEPTH) ** 0.25
DEEPNORM_BETA = (8.0 * DEPTH) ** -0.25
SB_BIAS_NEAR = -3.0
SB_BIAS_FAR = -10.0

kernel_name = "yoco_pool_stickbreak_decoder_step"


def _layer_norm(x, g, b):
    xf = x.astype(jnp.float32)
    mu = jnp.mean(xf, axis=-1, keepdims=True)
    xc = xf - mu
    var = jnp.mean(xc * xc, axis=-1, keepdims=True)
    return (xc * lax.rsqrt(var + LN_EPS) * g.astype(jnp.float32) + b.astype(jnp.float32)).astype(x.dtype)


def _mlp(h, w_up, b_up, w_down, b_down):
    a = jax.nn.relu(h @ w_up + b_up)
    return (a * a) @ w_down + b_down


def _pool_mixer(u, n_out, w_pool, b_pool, scale):
    b, total, _ = u.shape
    P = total - n_out
    uf = u.astype(jnp.float32)
    cs = jnp.concatenate([jnp.zeros((b, 1, D_MODEL), jnp.float32), jnp.cumsum(uf, axis=1)], axis=1)
    hi = jnp.arange(P, total) + 1
    outs = []
    for g, w in enumerate(POOL_WINDOWS):
        lo = jnp.maximum(hi - w, 0)
        c0, c1 = g * POOL_GROUP, (g + 1) * POOL_GROUP
        s = cs[:, hi, c0:c1] - cs[:, lo, c0:c1]
        cnt = (hi - lo).astype(jnp.float32)[None, :, None]
        outs.append(s / cnt - uf[:, P:, c0:c1])
    p = jnp.stack(outs, axis=2).astype(u.dtype)
    y = jnp.einsum('blgc,gcd->blgd', p, w_pool).reshape(b, n_out, D_MODEL)
    return (y + b_pool) * scale


def _stick_breaking(q, k, v, q_pos, k_pos, bias):
    z = (jnp.einsum('bqhd,bkhd->bhqk', q, k).astype(jnp.float32) * (HEAD_DIM ** -0.5)
         + bias.astype(jnp.float32)[None, :, None, None])
    visible = (k_pos[None, :] < q_pos[:, None])[None, None]
    log_beta = jax.nn.log_sigmoid(z)
    log_1mb = jnp.where(visible, jax.nn.log_sigmoid(-z), 0.0)
    suffix = lax.cumsum(log_1mb, axis=3, reverse=True) - log_1mb
    A = jnp.exp(jnp.where(visible, log_beta + suffix, -jnp.inf))
    return jnp.einsum('bhqk,bkhd->bqhd', A.astype(v.dtype), v)


def _trunk(h, pool_ctx, attend, ln_g, ln_b, w_pool, b_pool, pool_scale,
           w_up, b_up, w_down, b_down, w_kv, w_q, w_o, sb_bias):
    b, n_out, _ = h.shape
    new_pool = []
    k = v = None
    for layer in range(DEPTH):
        if layer < N_A_LAYERS:
            u = h if pool_ctx is None else jnp.concatenate([pool_ctx[layer].astype(h.dtype), h], axis=1)
            new_pool.append(u[:, -POOL_CTX:])
            mix = _pool_mixer(u, n_out, w_pool[layer], b_pool[layer], pool_scale[layer])
        else:
            j = layer - N_A_LAYERS
            q = (h @ w_q[j]).reshape(b, n_out, N_HEADS, HEAD_DIM)
            o = attend(q, k, v, sb_bias[j])
            mix = o.reshape(b, n_out, D_MODEL) @ w_o[j]
        h = _layer_norm(DEEPNORM_ALPHA * h + mix, ln_g[layer, 0], ln_b[layer, 0])
        h = _layer_norm(DEEPNORM_ALPHA * h + _mlp(h, w_up[layer], b_up[layer], w_down[layer], b_down[layer]),
                        ln_g[layer, 1], ln_b[layer, 1])
        if layer == N_A_LAYERS - 1:
            kv = h @ w_kv
            k = kv[..., :D_MODEL].reshape(b, n_out, N_HEADS, HEAD_DIM)
            v = kv[..., D_MODEL:].reshape(b, n_out, N_HEADS, HEAD_DIM)
    return h, jnp.stack(new_pool), k, v


def setup_inputs(seed: int = 0) -> dict:
    key = jax.random.key(seed)
    ks = jax.random.split(key, 20)
    n_pages = PAST_LEN // PAGE_SIZE
    n_used = DEC_BATCH * n_pages
    n_phys = (5 * n_used + 3) // 4
    page_table = jax.random.permutation(ks[0], n_phys)[:n_used].reshape(DEC_BATCH, n_pages).astype(jnp.int32)
    nrm = lambda k, s: jax.random.normal(k, s, jnp.float32)
    v_col_scale = jnp.concatenate([jnp.ones((D_MODEL,), jnp.float32),
                                   jnp.full((D_MODEL,), DEEPNORM_BETA, jnp.float32)])
    sb_bias = (jnp.linspace(SB_BIAS_NEAR, SB_BIAS_FAR, N_HEADS, dtype=jnp.float32)[None]
               + 0.1 * nrm(ks[19], (N_B_LAYERS, N_HEADS)))
    return {
        "x_prompt": nrm(ks[1], (BATCH, SEQ, D_MODEL)),
        "x_sample": nrm(ks[2], (DEC_BATCH, DEC_SEQ, D_MODEL)),
        "cache_k": nrm(ks[3], (n_phys, PAGE_SIZE, N_HEADS, HEAD_DIM)),
        "cache_v": nrm(ks[4], (n_phys, PAGE_SIZE, N_HEADS, HEAD_DIM)) * DEEPNORM_BETA,
        "state_pool": nrm(ks[5], (N_A_LAYERS, DEC_BATCH, POOL_CTX, D_MODEL)),
        "page_table": page_table,
        "meta_tokens": nrm(ks[6], (N_META, D_MODEL)),
        "ln_g": 1.0 + 0.05 * nrm(ks[7], (DEPTH, 2, D_MODEL)),
        "ln_b": 0.02 * nrm(ks[8], (DEPTH, 2, D_MODEL)),
        "w_pool": nrm(ks[9], (N_A_LAYERS, N_POOL_GROUPS, POOL_GROUP, POOL_GROUP)) * (POOL_GROUP ** -0.5) * DEEPNORM_BETA,
        "b_pool": 0.02 * nrm(ks[10], (N_A_LAYERS, D_MODEL)),
        "pool_scale": 1.0 + 0.1 * nrm(ks[11], (N_A_LAYERS, D_MODEL)),
        "w_up": nrm(ks[12], (DEPTH, D_MODEL, D_FF)) * (D_MODEL ** -0.5),
        "b_up": 0.02 * nrm(ks[13], (DEPTH, D_FF)),
        "w_down": nrm(ks[14], (DEPTH, D_FF, D_MODEL)) * (D_FF ** -0.5) * DEEPNORM_BETA,
        "b_down": 0.02 * nrm(ks[15], (DEPTH, D_MODEL)),
        "w_kv": nrm(ks[16], (D_MODEL, 2 * D_MODEL)) * (D_MODEL ** -0.5) * v_col_scale,
        "w_q": nrm(ks[17], (N_B_LAYERS, D_MODEL, D_MODEL)) * (D_MODEL ** -0.5),
        "w_o": nrm(ks[18], (N_B_LAYERS, D_MODEL, D_MODEL)) * (D_MODEL ** -0.5) * DEEPNORM_BETA,
        "sb_bias": sb_bias,
    }


def reference(x_prompt, x_sample, cache_k, cache_v, state_pool, page_table, meta_tokens,
              ln_g, ln_b, w_pool, b_pool, pool_scale, w_up, b_up, w_down, b_down, w_kv, w_q, w_o, sb_bias):
    weights = (ln_g, ln_b, w_pool, b_pool, pool_scale, w_up, b_up, w_down, b_down, w_kv, w_q, w_o, sb_bias)

    bp = x_prompt.shape[0]
    h0 = jnp.concatenate([jnp.broadcast_to(meta_tokens.astype(x_prompt.dtype)[None], (bp, N_META, D_MODEL)),
                          x_prompt], axis=1)

    def attend_prompt(q, k, v, bias):
        T = k.shape[1]
        n_real = T - N_META
        n_blk = n_real // Q_BLOCK
        k_pos = jnp.arange(T)
        meta_pos = jnp.arange(N_META)
        o_meta = _stick_breaking(q[:, :N_META], k[:, :N_META], v[:, :N_META], meta_pos, meta_pos, bias)
        q_blocks = q[:, N_META:].reshape(bp, n_blk, Q_BLOCK, N_HEADS, HEAD_DIM).transpose(1, 0, 2, 3, 4)
        q_pos = (N_META + jnp.arange(n_real)).reshape(n_blk, Q_BLOCK)
        o_blocks = lax.map(lambda a: _stick_breaking(a[0], k, v, a[1], k_pos, bias), (q_blocks, q_pos))
        o_real = o_blocks.transpose(1, 0, 2, 3, 4).reshape(bp, n_real, N_HEADS, HEAD_DIM)
        return jnp.concatenate([o_meta, o_real], axis=1)

    h_p, pool_prompt, k_rows_prompt, v_rows_prompt = _trunk(h0, None, attend_prompt, *weights)
    y_prompt = h_p[:, N_META:]

    bd, n_new, _ = x_sample.shape
    past = page_table.shape[1] * PAGE_SIZE

    def attend_sample(q, k, v, bias):
        k_past = cache_k[page_table].reshape(bd, past, N_HEADS, HEAD_DIM).astype(k.dtype)
        v_past = cache_v[page_table].reshape(bd, past, N_HEADS, HEAD_DIM).astype(v.dtype)
        k_all = jnp.concatenate([k_past, k], axis=1)
        v_all = jnp.concatenate([v_past, v], axis=1)
        k_pos = jnp.arange(past + n_new)
        q_pos = past + jnp.arange(n_new)
        return _stick_breaking(q, k_all, v_all, q_pos, k_pos, bias)

    y_sample, pool_sample, k_rows_sample, v_rows_sample = _trunk(x_sample, state_pool, attend_sample, *weights)

    return (y_prompt, y_sample, k_rows_prompt, v_rows_prompt, k_rows_sample, v_rows_sample, pool_prompt, pool_sample)
```

```python
import functools

import jax
import jax.numpy as jnp
from jax import lax
from jax.experimental import pallas as pl
from jax.experimental.pallas import tpu as pltpu

D_MODEL = 1024
N_META = 16
POOL_WINDOWS = (2, 4, 8, 16)
POOL_GROUP = D_MODEL // len(POOL_WINDOWS)
POOL_CTX = max(POOL_WINDOWS) - 1
HALO = 16
HEAD_DIM = 64
N_HEADS = D_MODEL // HEAD_DIM
D_FF = 4 * D_MODEL
LN_EPS = 1e-5
DEPTH = 2
DEEPNORM_ALPHA = (2.0 * DEPTH) ** 0.25
PAGE_SIZE = 128

ROW_TILE = 256
ATT_TQ = 256
ATT_TK = 256
LANES = 128
VMEM_LIMIT_BYTES = 56 * 1024 * 1024

F32 = jnp.float32
BF16 = jnp.bfloat16


def _layer_norm(x, g, b):
    mu = jnp.mean(x, axis=-1, keepdims=True)
    xc = x - mu
    var = jnp.mean(xc * xc, axis=-1, keepdims=True)
    return xc * lax.rsqrt(var + LN_EPS) * g + b


def _mlp(hb, wup_ref, bup_ref, wdown_ref):
    acc = None
    for c in range(D_FF // D_MODEL):
        lo, hi = c * D_MODEL, (c + 1) * D_MODEL
        a = jnp.dot(hb, wup_ref[:, lo:hi], preferred_element_type=F32) + bup_ref[:, lo:hi]
        a = jnp.maximum(a, 0.0)
        a2 = (a * a).astype(BF16)
        part = jnp.dot(a2, wdown_ref[lo:hi, :], preferred_element_type=F32)
        acc = part if acc is None else acc + part
    return acc


def _layer0_kernel(prev_ref, cur_ref, wpool_ref, bpool_ref, pscale_ref, lng_ref, lnb_ref,
                   wup_ref, bup_ref, wdown_ref, bdown_ref, wkt_ref, wvt_ref, wq_ref,
                   h_out, kt_out, vt_out, qb_out, ktb_out, vtb_out, ext_ref, *, prompt, tm):
    i = pl.program_id(1)
    cur = cur_ref[0]
    if prompt:
        prev = jnp.where(i == 0, 0.0, prev_ref[0])
        pos = i * tm + lax.broadcasted_iota(jnp.int32, (tm, 1), 0)
    else:
        prev = jnp.zeros((HALO, D_MODEL), F32)
        pos = None
    ext_ref[0:HALO, :] = prev
    ext_ref[HALO:HALO + tm, :] = cur

    ys = []
    for g, w in enumerate(POOL_WINDOWS):
        c0, c1 = g * POOL_GROUP, (g + 1) * POOL_GROUP
        s = cur[:, c0:c1]
        for k in range(1, w):
            s = s + ext_ref[HALO - k:HALO - k + tm, c0:c1]
        if prompt:
            inv_cnt = 1.0 / jnp.minimum(pos + 1, w).astype(F32)
        else:
            inv_cnt = 1.0 / w
        p = s * inv_cnt - cur[:, c0:c1]
        ys.append(jnp.dot(p.astype(BF16), wpool_ref[g], preferred_element_type=F32))
    mix = (jnp.concatenate(ys, axis=-1) + bpool_ref[...]) * pscale_ref[...]

    h1 = _layer_norm(DEEPNORM_ALPHA * cur + mix, lng_ref[0:1, :], lnb_ref[0:1, :])
    m = _mlp(h1.astype(BF16), wup_ref, bup_ref, wdown_ref) + bdown_ref[...]
    h2 = _layer_norm(DEEPNORM_ALPHA * h1 + m, lng_ref[1:2, :], lnb_ref[1:2, :])
    h_out[0] = h2

    h2b = h2.astype(BF16)
    nt_dims = (((1,), (1,)), ((), ()))
    kt = lax.dot_general(wkt_ref[...], h2b, nt_dims, preferred_element_type=F32)
    kt_out[0] = kt
    ktb_out[0, 0] = kt.astype(BF16)
    vt = lax.dot_general(wvt_ref[...], h2b, nt_dims, preferred_element_type=F32)
    vt_out[0] = vt
    vtb_out[0, 0] = vt.astype(BF16)
    q = jnp.dot(h2b, wq_ref[...], preferred_element_type=F32)
    qb_out[0] = (q * (HEAD_DIM ** -0.5)).astype(BF16)


def _const_spec(shape):
    nd = len(shape)
    return pl.BlockSpec(shape, lambda *_: (0,) * nd, pipeline_mode=pl.Buffered(1))


def _layer0_call(u, n_rows_out, wts, *, prompt):
    nb, rows, _ = u.shape
    tm = ROW_TILE
    nt = rows // tm
    halo_blocks = tm // HALO
    row_spec = pl.BlockSpec((1, tm, D_MODEL), lambda b, i: (b, i, 0))
    prev_spec = pl.BlockSpec((1, HALO, D_MODEL), lambda b, i: (b, jnp.maximum(i * halo_blocks - 1, 0), 0))
    col_spec = pl.BlockSpec((1, D_MODEL, tm), lambda b, i: (b, 0, i))
    tile_spec = pl.BlockSpec((1, 1, D_MODEL, tm), lambda b, i: (b, i, 0, 0))
    in_specs = [prev_spec, row_spec,
                _const_spec((len(POOL_WINDOWS), POOL_GROUP, POOL_GROUP)),
                _const_spec((1, D_MODEL)), _const_spec((1, D_MODEL)),
                _const_spec((2, D_MODEL)), _const_spec((2, D_MODEL)),
                _const_spec((D_MODEL, D_FF)), _const_spec((1, D_FF)),
                _const_spec((D_FF, D_MODEL)), _const_spec((1, D_MODEL)),
                _const_spec((D_MODEL, D_MODEL)), _const_spec((D_MODEL, D_MODEL)),
                _const_spec((D_MODEL, D_MODEL))]
    out_shape = (jax.ShapeDtypeStruct((nb, rows, D_MODEL), F32),
                 jax.ShapeDtypeStruct((nb, D_MODEL, n_rows_out), F32),
                 jax.ShapeDtypeStruct((nb, D_MODEL, n_rows_out), F32),
                 jax.ShapeDtypeStruct((nb, rows, D_MODEL), BF16),
                 jax.ShapeDtypeStruct((nb, nt, D_MODEL, tm), BF16),
                 jax.ShapeDtypeStruct((nb, nt, D_MODEL, tm), BF16))
    return pl.pallas_call(
        functools.partial(_layer0_kernel, prompt=prompt, tm=tm),
        grid=(nb, nt),
        in_specs=in_specs,
        out_specs=(row_spec, col_spec, col_spec, row_spec, tile_spec, tile_spec),
        out_shape=out_shape,
        scratch_shapes=[pltpu.VMEM((HALO + tm, D_MODEL), F32)],
        compiler_params=pltpu.CompilerParams(dimension_semantics=("parallel", "arbitrary"),
                                             vmem_limit_bytes=VMEM_LIMIT_BYTES),
        name="layer0",
    )(u, u, *wts)


def _layer1_kernel(o_ref, h_ref, wo_ref, lng_ref, lnb_ref, wup_ref, bup_ref, wdown_ref, bdown_ref, y_out):
    h = h_ref[...]
    mix = jnp.dot(o_ref[...], wo_ref[...], preferred_element_type=F32)
    h1 = _layer_norm(DEEPNORM_ALPHA * h + mix, lng_ref[0:1, :], lnb_ref[0:1, :])
    m = _mlp(h1.astype(BF16), wup_ref, bup_ref, wdown_ref) + bdown_ref[...]
    y_out[...] = _layer_norm(DEEPNORM_ALPHA * h1 + m, lng_ref[1:2, :], lnb_ref[1:2, :])


def _layer1_call(o, h, wts, tm):
    rows = h.shape[0]
    row_spec = pl.BlockSpec((tm, D_MODEL), lambda i: (i, 0))
    in_specs = [row_spec, row_spec,
                _const_spec((D_MODEL, D_MODEL)),
                _const_spec((2, D_MODEL)), _const_spec((2, D_MODEL)),
                _const_spec((D_MODEL, D_FF)), _const_spec((1, D_FF)),
                _const_spec((D_FF, D_MODEL)), _const_spec((1, D_MODEL))]
    return pl.pallas_call(
        _layer1_kernel,
        grid=(rows // tm,),
        in_specs=in_specs,
        out_specs=row_spec,
        out_shape=jax.ShapeDtypeStruct((rows, D_MODEL), F32),
        compiler_params=pltpu.CompilerParams(dimension_semantics=("parallel",),
                                             vmem_limit_bytes=VMEM_LIMIT_BYTES),
        name="layer1",
    )(o, h, *wts)


def _sb_tile(z, tri, carry, mask):
    sp = jnp.maximum(z, 0.0) + jnp.log(1.0 + jnp.exp(-jnp.abs(z)))
    if mask is not None:
        sp = jnp.where(mask, sp, 0.0)
    hi = sp.astype(BF16)
    lo = (sp - hi.astype(F32)).astype(BF16)
    suffix = (jnp.dot(hi, tri, preferred_element_type=F32)
              + jnp.dot(lo, tri, preferred_element_type=F32) + carry)
    a = jnp.exp(z - suffix)
    if mask is not None:
        a = jnp.where(mask, a, 0.0)
    return a.astype(BF16), suffix[:, 0:1]


def _attn_prompt_kernel(bias_ref, q_ref, k_ref, v_ref, tri_ref, o_ref, acc_ref):
    hp = pl.program_id(1)
    i = pl.program_id(2)
    tq, tk = ATT_TQ, ATT_TK
    q = q_ref[0]
    lane = lax.broadcasted_iota(jnp.int32, (tq, LANES), 1)
    head_of_lane = lane // HEAD_DIM
    qh = [jnp.where(head_of_lane == h, q, jnp.zeros_like(q)) for h in range(2)]
    bias = [bias_ref[2 * hp + h] for h in range(2)]
    tri = tri_ref[...]
    acc_ref[...] = jnp.zeros_like(acc_ref)

    def tile(t, carries, masked):
        kt = k_ref[0, t]
        vt = v_ref[0, t]
        if masked:
            row = lax.broadcasted_iota(jnp.int32, (tq, tk), 0)
            col = lax.broadcasted_iota(jnp.int32, (tq, tk), 1)
            mask = col < row
        else:
            mask = None
        new = []
        for h in range(2):
            z = jnp.dot(qh[h], kt, preferred_element_type=F32) + bias[h]
            a, c = _sb_tile(z, tri, carries[h], mask)
            acc_ref[h] += lax.dot_general(a, vt, (((1,), (1,)), ((), ())), preferred_element_type=F32)
            new.append(c)
        return tuple(new)

    zero = jnp.zeros((tq, 1), F32)
    carries = tile(i, (zero, zero), True)
    lax.fori_loop(0, i, lambda j, c: tile(i - 1 - j, c, False), carries)
    o_ref[0] = jnp.where(head_of_lane == 0, acc_ref[0], acc_ref[1]).astype(o_ref.dtype)


def _attn_prompt_call(bias, qb, ktb, vtb, tri):
    nb, rows, _ = qb.shape
    nq = rows // ATT_TQ
    nk = ktb.shape[1]
    q_spec = pl.BlockSpec((1, ATT_TQ, LANES), lambda b, hp, i: (b, i, hp))
    kv_spec = pl.BlockSpec((1, nk, LANES, ATT_TK), lambda b, hp, i: (b, 0, hp, 0))
    return pl.pallas_call(
        _attn_prompt_kernel,
        grid=(nb, N_HEADS // 2, nq),
        in_specs=[pl.BlockSpec(memory_space=pltpu.SMEM), q_spec, kv_spec, kv_spec,
                  pl.BlockSpec((ATT_TK, ATT_TK), lambda b, hp, i: (0, 0))],
        out_specs=q_spec,
        out_shape=jax.ShapeDtypeStruct((nb, rows, D_MODEL), BF16),
        scratch_shapes=[pltpu.VMEM((2, ATT_TQ, LANES), F32)],
        compiler_params=pltpu.CompilerParams(dimension_semantics=("parallel", "parallel", "arbitrary"),
                                             vmem_limit_bytes=VMEM_LIMIT_BYTES),
        name="attn_prompt",
    )(bias, qb, ktb, vtb, tri)


def _attn_sample_kernel(pt_ref, qbd_ref, bias_ref, knew_ref, vnew_ref, kpage_ref, vpage_ref, tri_ref, sel_ref,
                        o_ref, acc_ref, carry_ref, *, n_new):
    del pt_ref
    j = pl.program_id(1)
    nrow = N_HEADS * n_new
    qbd = qbd_ref[0]
    tri = tri_ref[...]

    def step(kt_blk, vt_blk, carry, mask):
        z = jnp.dot(qbd, kt_blk, preferred_element_type=F32) + bias_ref[...]
        a, c = _sb_tile(z, tri, carry, mask)
        return lax.dot_general(a, vt_blk, (((1,), (1,)), ((), ())), preferred_element_type=F32), c

    @pl.when(j == 0)
    def _():
        qi = lax.broadcasted_iota(jnp.int32, (nrow, PAGE_SIZE), 0) // N_HEADS
        slot = lax.broadcasted_iota(jnp.int32, (nrow, PAGE_SIZE), 1)
        contrib, c = step(knew_ref[0], vnew_ref[0], jnp.zeros((nrow, 1), F32), slot < qi)
        acc_ref[...] = contrib
        carry_ref[...] = jnp.broadcast_to(c, carry_ref.shape)

    @pl.when(j > 0)
    def _():
        contrib, c = step(kpage_ref[0].astype(BF16), vpage_ref[0].astype(BF16), carry_ref[:, 0:1], None)
        acc_ref[...] += contrib
        carry_ref[...] = jnp.broadcast_to(c, carry_ref.shape)

    @pl.when(j == pl.num_programs(1) - 1)
    def _():
        picked = acc_ref[...] * sel_ref[...]
        o_ref[0] = jnp.sum(picked.reshape(n_new, N_HEADS, D_MODEL), axis=1).astype(o_ref.dtype)


def _attn_sample_call(page_table, qbd, bias_rows, knew, vnew, cache_k, cache_v, tri, sel, n_new):
    nseq, n_pages = page_table.shape
    nrow = N_HEADS * n_new

    def page_map(b, j, pt):
        return (pt[b, n_pages - jnp.maximum(j, 1)], 0, 0)

    seq3 = lambda b, j, pt: (b, 0, 0)
    const2 = lambda b, j, pt: (0, 0)
    grid_spec = pltpu.PrefetchScalarGridSpec(
        num_scalar_prefetch=1,
        grid=(nseq, n_pages + 1),
        in_specs=[pl.BlockSpec((1, nrow, D_MODEL), seq3),
                  pl.BlockSpec((nrow, PAGE_SIZE), const2),
                  pl.BlockSpec((1, D_MODEL, PAGE_SIZE), seq3),
                  pl.BlockSpec((1, D_MODEL, PAGE_SIZE), seq3),
                  pl.BlockSpec((1, D_MODEL, PAGE_SIZE), page_map),
                  pl.BlockSpec((1, D_MODEL, PAGE_SIZE), page_map),
                  pl.BlockSpec((PAGE_SIZE, PAGE_SIZE), const2),
                  pl.BlockSpec((nrow, D_MODEL), const2)],
        out_specs=pl.BlockSpec((1, n_new, D_MODEL), seq3),
        scratch_shapes=[pltpu.VMEM((nrow, D_MODEL), F32), pltpu.VMEM((nrow, LANES), F32)])
    return pl.pallas_call(
        functools.partial(_attn_sample_kernel, n_new=n_new),
        grid_spec=grid_spec,
        out_shape=jax.ShapeDtypeStruct((nseq, n_new, D_MODEL), BF16),
        compiler_params=pltpu.CompilerParams(dimension_semantics=("parallel", "arbitrary"),
                                             vmem_limit_bytes=VMEM_LIMIT_BYTES),
        name="attn_sample",
    )(page_table, qbd, bias_rows, knew, vnew, cache_k, cache_v, tri, sel)


def _suffix_matrix(n):
    j = jnp.arange(n)[:, None]
    s = jnp.arange(n)[None, :]
    return (j >= s).astype(BF16)


def kernel(x_prompt, x_sample, cache_k, cache_v, state_pool, page_table, meta_tokens, ln_g, ln_b, w_pool, b_pool,
           pool_scale, w_up, b_up, w_down, b_down, w_kv, w_q, w_o, sb_bias):
    nb, seq, _ = x_prompt.shape
    nseq, n_new, _ = x_sample.shape
    t_real = seq + N_META
    t_pad = -(-t_real // ATT_TK) * ATT_TK

    wts0 = (w_pool[0].astype(BF16), b_pool[0:1], pool_scale[0:1], ln_g[0], ln_b[0],
            w_up[0].astype(BF16), b_up[0:1], w_down[0].astype(BF16), b_down[0:1],
            w_kv[:, :D_MODEL].T.astype(BF16), w_kv[:, D_MODEL:].T.astype(BF16), w_q[0].astype(BF16))
    wts1 = (w_o[0].astype(BF16), ln_g[1], ln_b[1], w_up[1].astype(BF16), b_up[1:2],
            w_down[1].astype(BF16), b_down[1:2])
    bias = sb_bias[0].astype(F32)

    h0 = jnp.concatenate([jnp.broadcast_to(meta_tokens[None], (nb, N_META, D_MODEL)), x_prompt,
                          jnp.zeros((nb, t_pad - t_real, D_MODEL), F32)], axis=1)
    h_p, kt_p, vt_p, qb_p, ktb_p, vtb_p = _layer0_call(h0, t_real, wts0, prompt=True)
    o_p = _attn_prompt_call(bias, qb_p, ktb_p, vtb_p, _suffix_matrix(ATT_TK))
    y_p = _layer1_call(o_p.reshape(nb * t_pad, D_MODEL), h_p.reshape(nb * t_pad, D_MODEL), wts1, ROW_TILE)
    y_prompt = y_p.reshape(nb, t_pad, D_MODEL)[:, N_META:t_real]
    pool_prompt = x_prompt[None, :, seq - POOL_CTX:, :]

    slot = 2 * HALO
    ctx = state_pool[0]
    u_s = jnp.concatenate([jnp.zeros((nseq, HALO - POOL_CTX, D_MODEL), F32), ctx, x_sample,
                           jnp.zeros((nseq, HALO - n_new, D_MODEL), F32)], axis=1)
    h_s, kt_s, vt_s, qb_s, _, _ = _layer0_call(u_s.reshape(1, nseq * slot, D_MODEL), nseq * slot, wts0, prompt=False)
    take = lambda a: a.reshape(nseq, slot, D_MODEL)[:, HALO:HALO + n_new]
    h_s, qb_s = take(h_s), take(qb_s)
    take_t = lambda a: a.reshape(D_MODEL, nseq, slot)[:, :, HALO:HALO + n_new].transpose(1, 0, 2)
    kt_s, vt_s = take_t(kt_s), take_t(vt_s)

    head_of_lane = jnp.arange(D_MODEL) // HEAD_DIM
    sel = (head_of_lane[None, :] == (jnp.arange(N_HEADS * n_new) % N_HEADS)[:, None])
    qbd = jnp.where(sel[None], jnp.repeat(qb_s, N_HEADS, axis=1), jnp.zeros((), BF16))
    bias_rows = jnp.broadcast_to(jnp.tile(bias, n_new)[:, None], (N_HEADS * n_new, PAGE_SIZE))
    pad_new = lambda a: jnp.pad(a.astype(BF16), ((0, 0), (0, 0), (0, PAGE_SIZE - n_new)))
    n_phys = cache_k.shape[0]
    pages_t = lambda c: c.transpose(0, 2, 3, 1).reshape(n_phys, D_MODEL, PAGE_SIZE)
    o_s = _attn_sample_call(page_table, qbd, bias_rows, pad_new(kt_s), pad_new(vt_s),
                            pages_t(cache_k), pages_t(cache_v),
                            _suffix_matrix(PAGE_SIZE), sel.astype(F32), n_new)
    y_s = _layer1_call(o_s.reshape(nseq * n_new, D_MODEL), h_s.reshape(nseq * n_new, D_MODEL), wts1, nseq * n_new)
    y_sample = y_s.reshape(nseq, n_new, D_MODEL)
    pool_sample = jnp.concatenate([ctx, x_sample], axis=1)[None, :, n_new:, :]

    rows = lambda a: a.reshape(a.shape[0], N_HEADS, HEAD_DIM, a.shape[2]).transpose(0, 3, 1, 2)
    return (y_prompt, y_sample, rows(kt_p), rows(vt_p), rows(kt_s), rows(vt_s), pool_prompt, pool_sample)
```

```python
import functools

import jax
import jax.numpy as jnp
from jax import lax
from jax.experimental import pallas as pl
from jax.experimental.pallas import tpu as pltpu

D_MODEL = 1024
N_META = 16
POOL_WINDOWS = (2, 4, 8, 16)
POOL_GROUP = D_MODEL // len(POOL_WINDOWS)
POOL_CTX = max(POOL_WINDOWS) - 1
HALO = 16
HEAD_DIM = 64
N_HEADS = D_MODEL // HEAD_DIM
D_FF = 4 * D_MODEL
LN_EPS = 1e-5
DEPTH = 2
DEEPNORM_ALPHA = (2.0 * DEPTH) ** 0.25
PAGE_SIZE = 128

ROW_TILE = 256
ATT_TQ = 256
ATT_TK = 256
ATT_HEADS = 8
SAMPLE_PAGES = 8
LANES = 128
VMEM_LIMIT_BYTES = 56 * 1024 * 1024

F32 = jnp.float32
BF16 = jnp.bfloat16
_NT_DIMS = (((1,), (1,)), ((), ()))


def _layer_norm(x, g, b):
    mu = jnp.mean(x, axis=-1, keepdims=True)
    xc = x - mu
    var = jnp.mean(xc * xc, axis=-1, keepdims=True)
    return xc * lax.rsqrt(var + LN_EPS) * g + b


def _mlp(hb, wup_ref, bup_ref, wdown_ref):
    acc = None
    for c in range(D_FF // D_MODEL):
        lo, hi = c * D_MODEL, (c + 1) * D_MODEL
        a = jnp.dot(hb, wup_ref[:, lo:hi], preferred_element_type=F32) + bup_ref[:, lo:hi]
        a = jnp.maximum(a, 0.0)
        a2 = (a * a).astype(BF16)
        part = jnp.dot(a2, wdown_ref[lo:hi, :], preferred_element_type=F32)
        acc = part if acc is None else acc + part
    return acc


def _layer0_kernel(prev_ref, cur_ref, wpool_ref, bpool_ref, pscale_ref, lng_ref, lnb_ref,
                   wup_ref, bup_ref, wdown_ref, bdown_ref, wkt_ref, wvt_ref, wq_ref,
                   h_out, kt_out, vt_out, qb_out, ktb_out, vtb_out, ext_ref, *, prompt, tm):
    i = pl.program_id(1)
    cur = cur_ref[0]
    if prompt:
        prev = jnp.where(i == 0, 0.0, prev_ref[0])
        pos = i * tm + lax.broadcasted_iota(jnp.int32, (tm, 1), 0)
    else:
        prev = jnp.zeros((HALO, D_MODEL), F32)
        pos = None
    ext_ref[0:HALO, :] = prev
    ext_ref[HALO:HALO + tm, :] = cur

    ys = []
    for g, w in enumerate(POOL_WINDOWS):
        c0, c1 = g * POOL_GROUP, (g + 1) * POOL_GROUP
        s = cur[:, c0:c1]
        for k in range(1, w):
            s = s + ext_ref[HALO - k:HALO - k + tm, c0:c1]
        if prompt:
            inv_cnt = 1.0 / jnp.minimum(pos + 1, w).astype(F32)
        else:
            inv_cnt = 1.0 / w
        p = s * inv_cnt - cur[:, c0:c1]
        ys.append(jnp.dot(p.astype(BF16), wpool_ref[g], preferred_element_type=F32))
    mix = (jnp.concatenate(ys, axis=-1) + bpool_ref[...]) * pscale_ref[...]

    h1 = _layer_norm(DEEPNORM_ALPHA * cur + mix, lng_ref[0:1, :], lnb_ref[0:1, :])
    m = _mlp(h1.astype(BF16), wup_ref, bup_ref, wdown_ref) + bdown_ref[...]
    h2 = _layer_norm(DEEPNORM_ALPHA * h1 + m, lng_ref[1:2, :], lnb_ref[1:2, :])
    h_out[0] = h2

    h2b = h2.astype(BF16)
    kt = lax.dot_general(wkt_ref[...], h2b, _NT_DIMS, preferred_element_type=F32)
    kt_out[0] = kt
    ktb_out[0, 0] = kt.astype(BF16)
    vt = lax.dot_general(wvt_ref[...], h2b, _NT_DIMS, preferred_element_type=F32)
    vt_out[0] = vt
    vtb_out[0, 0] = vt.astype(BF16)
    q = jnp.dot(h2b, wq_ref[...], preferred_element_type=F32)
    qb_out[0] = (q * (HEAD_DIM ** -0.5)).astype(BF16)


def _const_spec(shape):
    nd = len(shape)
    return pl.BlockSpec(shape, lambda *_: (0,) * nd, pipeline_mode=pl.Buffered(1))


def _layer0_call(u, n_rows_out, wts, *, prompt):
    nb, rows, _ = u.shape
    tm = ROW_TILE
    nt = rows // tm
    halo_blocks = tm // HALO
    row_spec = pl.BlockSpec((1, tm, D_MODEL), lambda b, i: (b, i, 0))
    prev_spec = pl.BlockSpec((1, HALO, D_MODEL), lambda b, i: (b, jnp.maximum(i * halo_blocks - 1, 0), 0))
    col_spec = pl.BlockSpec((1, D_MODEL, tm), lambda b, i: (b, 0, i))
    tile_spec = pl.BlockSpec((1, 1, D_MODEL, tm), lambda b, i: (b, i, 0, 0))
    in_specs = [prev_spec, row_spec,
                _const_spec((len(POOL_WINDOWS), POOL_GROUP, POOL_GROUP)),
                _const_spec((1, D_MODEL)), _const_spec((1, D_MODEL)),
                _const_spec((2, D_MODEL)), _const_spec((2, D_MODEL)),
                _const_spec((D_MODEL, D_FF)), _const_spec((1, D_FF)),
                _const_spec((D_FF, D_MODEL)), _const_spec((1, D_MODEL)),
                _const_spec((D_MODEL, D_MODEL)), _const_spec((D_MODEL, D_MODEL)),
                _const_spec((D_MODEL, D_MODEL))]
    out_shape = (jax.ShapeDtypeStruct((nb, rows, D_MODEL), F32),
                 jax.ShapeDtypeStruct((nb, D_MODEL, n_rows_out), F32),
                 jax.ShapeDtypeStruct((nb, D_MODEL, n_rows_out), F32),
                 jax.ShapeDtypeStruct((nb, rows, D_MODEL), BF16),
                 jax.ShapeDtypeStruct((nb, nt, D_MODEL, tm), BF16),
                 jax.ShapeDtypeStruct((nb, nt, D_MODEL, tm), BF16))
    return pl.pallas_call(
        functools.partial(_layer0_kernel, prompt=prompt, tm=tm),
        grid=(nb, nt),
        in_specs=in_specs,
        out_specs=(row_spec, col_spec, col_spec, row_spec, tile_spec, tile_spec),
        out_shape=out_shape,
        scratch_shapes=[pltpu.VMEM((HALO + tm, D_MODEL), F32)],
        compiler_params=pltpu.CompilerParams(dimension_semantics=("parallel", "arbitrary"),
                                             vmem_limit_bytes=VMEM_LIMIT_BYTES),
        name="layer0",
    )(u, u, *wts)


def _layer1_kernel(o_ref, h_ref, wo_ref, lng_ref, lnb_ref, wup_ref, bup_ref, wdown_ref, bdown_ref, y_out):
    h = h_ref[...]
    mix = jnp.dot(o_ref[...], wo_ref[...], preferred_element_type=F32)
    h1 = _layer_norm(DEEPNORM_ALPHA * h + mix, lng_ref[0:1, :], lnb_ref[0:1, :])
    m = _mlp(h1.astype(BF16), wup_ref, bup_ref, wdown_ref) + bdown_ref[...]
    y_out[...] = _layer_norm(DEEPNORM_ALPHA * h1 + m, lng_ref[1:2, :], lnb_ref[1:2, :])


def _layer1_call(o, h, wts, tm):
    rows = h.shape[0]
    row_spec = pl.BlockSpec((tm, D_MODEL), lambda i: (i, 0))
    in_specs = [row_spec, row_spec,
                _const_spec((D_MODEL, D_MODEL)),
                _const_spec((2, D_MODEL)), _const_spec((2, D_MODEL)),
                _const_spec((D_MODEL, D_FF)), _const_spec((1, D_FF)),
                _const_spec((D_FF, D_MODEL)), _const_spec((1, D_MODEL))]
    return pl.pallas_call(
        _layer1_kernel,
        grid=(rows // tm,),
        in_specs=in_specs,
        out_specs=row_spec,
        out_shape=jax.ShapeDtypeStruct((rows, D_MODEL), F32),
        compiler_params=pltpu.CompilerParams(dimension_semantics=("parallel",),
                                             vmem_limit_bytes=VMEM_LIMIT_BYTES),
        name="layer1",
    )(o, h, *wts)


def _softplus(z):
    return jnp.maximum(z, 0.0) + jnp.log(1.0 + jnp.exp(-jnp.abs(z)))


def _suffix_sum(sp, tri):
    return jnp.dot(sp.astype(BF16), tri, preferred_element_type=F32)


def _attn_prompt_kernel(bias_ref, q_ref, k_ref, v_ref, tri_ref, o_ref, qh_buf, z_buf, a_buf, acc_ref, carry_ref):
    g = pl.program_id(1)
    i = pl.program_id(2)
    tq, tk, nh = ATT_TQ, ATT_TK, ATT_HEADS
    lane = lax.broadcasted_iota(jnp.int32, (tq, LANES), 1)
    pair_sub = [divmod(h, 2) for h in range(nh)]
    for h, (pr, sub) in enumerate(pair_sub):
        qp = q_ref[0, :, pr * LANES:(pr + 1) * LANES]
        qh_buf[h] = jnp.where(lane // HEAD_DIM == sub, qp, jnp.zeros_like(qp))
    acc_ref[...] = jnp.zeros_like(acc_ref)
    carry_ref[...] = jnp.zeros_like(carry_ref)

    def key_tile(p):
        return jnp.maximum(i - p, 0)

    def qk(h, t):
        pr = pair_sub[h][0]
        kt = k_ref[0, t, pr * LANES:(pr + 1) * LANES, :]
        return jnp.dot(qh_buf[h], kt, preferred_element_type=F32)

    def av(h, t):
        pr = pair_sub[h][0]
        vt = v_ref[0, t, pr * LANES:(pr + 1) * LANES, :]
        return lax.dot_general(a_buf[h], vt, _NT_DIMS, preferred_element_type=F32)

    def weights(h, z, s, mask):
        c = carry_ref[h]
        suffix = s + jnp.concatenate([c] * (tk // LANES), axis=1)
        a = jnp.exp(z - suffix)
        if mask is not None:
            a = jnp.where(mask, a, 0.0)
        a_buf[h] = a.astype(BF16)
        carry_ref[h] = jnp.broadcast_to(suffix[:, 0:1], (tq, LANES))

    def sweep(p1, p3, do1, do2, do3, masked):
        tri = tri_ref[...]
        mask = None
        if masked:
            row = lax.broadcasted_iota(jnp.int32, (tq, tk), 0)
            col = lax.broadcasted_iota(jnp.int32, (tq, tk), 1)
            mask = col < row
        t1, t3 = key_tile(p1), key_tile(p3)
        zs, ss, avs, qks = {}, {}, {}, {}
        for k in range(nh + 2):
            if k < nh:
                if do3:
                    avs[k] = av(k, t3)
                if do2:
                    z = z_buf[k]
                    sp = _softplus(z)
                    if masked:
                        sp = jnp.where(mask, sp, 0.0)
                    zs[k], ss[k] = z, _suffix_sum(sp, tri)
            h = k - 1
            if 0 <= h < nh:
                if do3:
                    acc_ref[h] += avs.pop(h)
                if do2:
                    weights(h, zs.pop(h), ss.pop(h), mask)
                if do1:
                    qks[h] = qk(h, t1)
            h = k - 2
            if 0 <= h < nh and do1:
                z_buf[h] = qks.pop(h) + bias_ref[nh * g + h]

    sweep(0, 0, True, False, False, False)
    sweep(1, 0, True, True, False, True)

    def body(n, _):
        sweep(n, n - 2, True, True, True, False)
        return 0

    lax.fori_loop(2, i + 1, body, 0)

    @pl.when(i >= 1)
    def _():
        sweep(0, i - 1, False, True, True, False)

    sweep(0, i, False, False, True, False)
    for pr in range(nh // 2):
        o_ref[0, :, pr * LANES:(pr + 1) * LANES] = jnp.where(
            lane // HEAD_DIM == 0, acc_ref[2 * pr], acc_ref[2 * pr + 1]).astype(o_ref.dtype)


def _attn_prompt_call(bias, qb, ktb, vtb, tri):
    nb, rows, _ = qb.shape
    nq = rows // ATT_TQ
    nk = ktb.shape[1]
    width = ATT_HEADS * HEAD_DIM
    q_spec = pl.BlockSpec((1, ATT_TQ, width), lambda b, g, i: (b, i, g))
    kv_spec = pl.BlockSpec((1, nk, width, ATT_TK), lambda b, g, i: (b, 0, g, 0))
    return pl.pallas_call(
        _attn_prompt_kernel,
        grid=(nb, N_HEADS // ATT_HEADS, nq),
        in_specs=[pl.BlockSpec(memory_space=pltpu.SMEM), q_spec, kv_spec, kv_spec,
                  pl.BlockSpec((ATT_TK, ATT_TK), lambda b, g, i: (0, 0))],
        out_specs=q_spec,
        out_shape=jax.ShapeDtypeStruct((nb, rows, D_MODEL), BF16),
        scratch_shapes=[pltpu.VMEM((ATT_HEADS, ATT_TQ, LANES), BF16),
                        pltpu.VMEM((ATT_HEADS, ATT_TQ, ATT_TK), F32),
                        pltpu.VMEM((ATT_HEADS, ATT_TQ, ATT_TK), BF16),
                        pltpu.VMEM((ATT_HEADS, ATT_TQ, LANES), F32),
                        pltpu.VMEM((ATT_HEADS, ATT_TQ, LANES), F32)],
        compiler_params=pltpu.CompilerParams(dimension_semantics=("parallel", "parallel", "arbitrary"),
                                             vmem_limit_bytes=VMEM_LIMIT_BYTES),
        name="attn_prompt",
    )(bias, qb, ktb, vtb, tri)


def _attn_sample_kernel(pt_ref, qbd_ref, bias_ref, knew_ref, vnew_ref, *refs, n_new, npp):
    del pt_ref
    kpages, vpages = refs[:npp], refs[npp:2 * npp]
    tri_ref, selt_ref, fold_ref, o_ref, kcat, vcat, a_buf, acc_t, carry_ref = refs[2 * npp:]
    j = pl.program_id(1)
    nrow = N_HEADS * n_new
    chunk = ATT_TK
    n_chunks = npp * PAGE_SIZE // chunk
    qbd = qbd_ref[0]
    tri = tri_ref[...]

    @pl.when(j == 0)
    def _():
        qi = lax.broadcasted_iota(jnp.int32, (nrow, PAGE_SIZE), 0) // N_HEADS
        slot = lax.broadcasted_iota(jnp.int32, (nrow, PAGE_SIZE), 1)
        mask = slot < qi
        z = jnp.dot(qbd, knew_ref[0], preferred_element_type=F32) + bias_ref[:, 0:PAGE_SIZE]
        sp = jnp.where(mask, _softplus(z), 0.0)
        suffix = _suffix_sum(sp, tri[0:PAGE_SIZE, 0:PAGE_SIZE])
        a = jnp.where(mask, jnp.exp(z - suffix), 0.0).astype(BF16)
        acc_t[...] = lax.dot_general(vnew_ref[0], a, _NT_DIMS, preferred_element_type=F32)
        carry_ref[...] = jnp.broadcast_to(suffix[:, 0:1], carry_ref.shape)
        a_buf[...] = jnp.zeros_like(a_buf)

    for k in range(npp):
        vcat[:, k * PAGE_SIZE:(k + 1) * PAGE_SIZE] = vpages[k][0].astype(BF16)
    acc_t[...] += lax.dot_general(vcat[...], a_buf[...], _NT_DIMS, preferred_element_type=F32)

    for k in range(npp):
        kcat[:, k * PAGE_SIZE:(k + 1) * PAGE_SIZE] = kpages[k][0].astype(BF16)
    z = jnp.dot(qbd, kcat[...], preferred_element_type=F32) + bias_ref[...]
    sp = _softplus(z)
    carry = carry_ref[...]
    for c in reversed(range(n_chunks)):
        lanes = slice(c * chunk, (c + 1) * chunk)
        suffix = _suffix_sum(sp[:, lanes], tri) + jnp.concatenate([carry] * (chunk // LANES), axis=1)
        a_buf[:, lanes] = jnp.exp(z[:, lanes] - suffix).astype(BF16)
        carry = jnp.broadcast_to(suffix[:, 0:1], carry.shape)
    carry_ref[...] = carry

    @pl.when(j == pl.num_programs(1) - 1)
    def _():
        picked = (acc_t[...] * selt_ref[...]).astype(BF16)
        o = lax.dot_general(fold_ref[...], picked, _NT_DIMS, preferred_element_type=F32)
        o_ref[0] = o[0:n_new].astype(o_ref.dtype)


def _attn_sample_call(page_table, qbd, bias_rows, knew, vnew, cache_kt, cache_vt, tri, selt, fold, n_new):
    nseq, n_pages = page_table.shape
    npp = SAMPLE_PAGES
    nsteps = n_pages // npp
    nrow = N_HEADS * n_new

    def page_map(k, shift):
        def index_map(b, j, pt):
            step = jnp.clip(j - shift, 0, nsteps - 1)
            return (pt[b, n_pages - (step + 1) * npp + k], 0, 0)
        return index_map

    seq3 = lambda b, j, pt: (b, 0, 0)
    const2 = lambda b, j, pt: (0, 0)
    page_block = (1, D_MODEL, PAGE_SIZE)
    grid_spec = pltpu.PrefetchScalarGridSpec(
        num_scalar_prefetch=1,
        grid=(nseq, nsteps + 1),
        in_specs=([pl.BlockSpec((1, nrow, D_MODEL), seq3),
                   pl.BlockSpec((nrow, npp * PAGE_SIZE), const2),
                   pl.BlockSpec(page_block, seq3),
                   pl.BlockSpec(page_block, seq3)]
                  + [pl.BlockSpec(page_block, page_map(k, 0)) for k in range(npp)]
                  + [pl.BlockSpec(page_block, page_map(k, 1)) for k in range(npp)]
                  + [pl.BlockSpec((ATT_TK, ATT_TK), const2),
                     pl.BlockSpec((D_MODEL, nrow), const2),
                     pl.BlockSpec((HALO, nrow), const2)]),
        out_specs=pl.BlockSpec((1, n_new, D_MODEL), seq3),
        scratch_shapes=[pltpu.VMEM((D_MODEL, npp * PAGE_SIZE), BF16),
                        pltpu.VMEM((D_MODEL, npp * PAGE_SIZE), BF16),
                        pltpu.VMEM((nrow, npp * PAGE_SIZE), BF16),
                        pltpu.VMEM((D_MODEL, nrow), F32),
                        pltpu.VMEM((nrow, LANES), F32)])
    return pl.pallas_call(
        functools.partial(_attn_sample_kernel, n_new=n_new, npp=npp),
        grid_spec=grid_spec,
        out_shape=jax.ShapeDtypeStruct((nseq, n_new, D_MODEL), BF16),
        compiler_params=pltpu.CompilerParams(dimension_semantics=("parallel", "arbitrary"),
                                             vmem_limit_bytes=VMEM_LIMIT_BYTES),
        name="attn_sample",
    )(page_table, qbd, bias_rows, knew, vnew, *([cache_kt] * npp), *([cache_vt] * npp), tri, selt, fold)


def _suffix_matrix(n):
    j = jnp.arange(n)[:, None]
    s = jnp.arange(n)[None, :]
    return (j >= s).astype(BF16)


def kernel(x_prompt, x_sample, cache_k, cache_v, state_pool, page_table, meta_tokens, ln_g, ln_b, w_pool, b_pool,
           pool_scale, w_up, b_up, w_down, b_down, w_kv, w_q, w_o, sb_bias):
    nb, seq, _ = x_prompt.shape
    nseq, n_new, _ = x_sample.shape
    t_real = seq + N_META
    t_pad = -(-t_real // ATT_TK) * ATT_TK
    assert n_new <= HALO and page_table.shape[1] % SAMPLE_PAGES == 0

    wts0 = (w_pool[0].astype(BF16), b_pool[0:1], pool_scale[0:1], ln_g[0], ln_b[0],
            w_up[0].astype(BF16), b_up[0:1], w_down[0].astype(BF16), b_down[0:1],
            w_kv[:, :D_MODEL].T.astype(BF16), w_kv[:, D_MODEL:].T.astype(BF16), w_q[0].astype(BF16))
    wts1 = (w_o[0].astype(BF16), ln_g[1], ln_b[1], w_up[1].astype(BF16), b_up[1:2],
            w_down[1].astype(BF16), b_down[1:2])
    bias = sb_bias[0].astype(F32)
    tri = _suffix_matrix(ATT_TK)

    h0 = jnp.concatenate([jnp.broadcast_to(meta_tokens[None], (nb, N_META, D_MODEL)), x_prompt,
                          jnp.zeros((nb, t_pad - t_real, D_MODEL), F32)], axis=1)
    h_p, kt_p, vt_p, qb_p, ktb_p, vtb_p = _layer0_call(h0, t_real, wts0, prompt=True)
    o_p = _attn_prompt_call(bias, qb_p, ktb_p, vtb_p, tri)
    y_p = _layer1_call(o_p.reshape(nb * t_pad, D_MODEL), h_p.reshape(nb * t_pad, D_MODEL), wts1, ROW_TILE)
    y_prompt = y_p.reshape(nb, t_pad, D_MODEL)[:, N_META:t_real]
    pool_prompt = x_prompt[None, :, seq - POOL_CTX:, :]

    slot = 2 * HALO
    ctx = state_pool[0]
    u_s = jnp.concatenate([jnp.zeros((nseq, HALO - POOL_CTX, D_MODEL), F32), ctx, x_sample,
                           jnp.zeros((nseq, HALO - n_new, D_MODEL), F32)], axis=1)
    h_s, kt_s, vt_s, qb_s, _, _ = _layer0_call(u_s.reshape(1, nseq * slot, D_MODEL), nseq * slot, wts0, prompt=False)
    take = lambda a: a.reshape(nseq, slot, D_MODEL)[:, HALO:HALO + n_new]
    h_s, qb_s = take(h_s), take(qb_s)
    take_t = lambda a: a.reshape(D_MODEL, nseq, slot)[:, :, HALO:HALO + n_new].transpose(1, 0, 2)
    kt_s, vt_s = take_t(kt_s), take_t(vt_s)

    nrow = N_HEADS * n_new
    head_of_lane = jnp.arange(D_MODEL) // HEAD_DIM
    sel = head_of_lane[None, :] == (jnp.arange(nrow) % N_HEADS)[:, None]
    qbd = jnp.where(sel[None], jnp.repeat(qb_s, N_HEADS, axis=1), jnp.zeros((), BF16))
    bias_rows = jnp.broadcast_to(jnp.tile(bias, n_new)[:, None], (nrow, SAMPLE_PAGES * PAGE_SIZE))
    fold = (jnp.arange(HALO)[:, None] == (jnp.arange(nrow) // N_HEADS)[None, :]).astype(BF16)
    pad_new = lambda a: jnp.pad(a.astype(BF16), ((0, 0), (0, 0), (0, PAGE_SIZE - n_new)))
    n_phys = cache_k.shape[0]
    pages_t = lambda c: c.transpose(0, 2, 3, 1).reshape(n_phys, D_MODEL, PAGE_SIZE)
    o_s = _attn_sample_call(page_table, qbd, bias_rows, pad_new(kt_s), pad_new(vt_s),
                            pages_t(cache_k), pages_t(cache_v), tri, sel.T.astype(F32), fold, n_new)
    y_s = _layer1_call(o_s.reshape(nseq * n_new, D_MODEL), h_s.reshape(nseq * n_new, D_MODEL), wts1, nseq * n_new)
    y_sample = y_s.reshape(nseq, n_new, D_MODEL)
    pool_sample = jnp.concatenate([ctx, x_sample], axis=1)[None, :, n_new:, :]

    rows = lambda a: a.reshape(a.shape[0], N_HEADS, HEAD_DIM, a.shape[2]).transpose(0, 3, 1, 2)
    return (y_prompt, y_sample, rows(kt_p), rows(vt_p), rows(kt_s), rows(vt_s), pool_prompt, pool_sample)
```

```python
import functools

import jax
import jax.numpy as jnp
from jax import lax
from jax.experimental import pallas as pl
from jax.experimental.pallas import tpu as pltpu

D_MODEL = 1024
N_META = 16
POOL_WINDOWS = (2, 4, 8, 16)
POOL_GROUP = D_MODEL // len(POOL_WINDOWS)
POOL_CTX = max(POOL_WINDOWS) - 1
HALO = 16
HEAD_DIM = 64
N_HEADS = D_MODEL // HEAD_DIM
D_FF = 4 * D_MODEL
LN_EPS = 1e-5
DEPTH = 2
DEEPNORM_ALPHA = (2.0 * DEPTH) ** 0.25
PAGE_SIZE = 128

ROW_TILE = 256
ATT_TQ = 256
ATT_TK = 256
ATT_HEADS = 8
SAMPLE_PAGES = 16
LANES = 128
BF16_SUBLANES = 16
VMEM_LIMIT_BYTES = 56 * 1024 * 1024

F32 = jnp.float32
BF16 = jnp.bfloat16
_NT_DIMS = (((1,), (1,)), ((), ()))


def _layer_norm(x, g, b):
    mu = jnp.mean(x, axis=-1, keepdims=True)
    xc = x - mu
    var = jnp.mean(xc * xc, axis=-1, keepdims=True)
    return xc * lax.rsqrt(var + LN_EPS) * g + b


def _mlp(hb, wup_ref, bup_ref, wdown_ref):
    acc = None
    for c in range(D_FF // D_MODEL):
        lo, hi = c * D_MODEL, (c + 1) * D_MODEL
        a = jnp.dot(hb, wup_ref[:, lo:hi], preferred_element_type=F32) + bup_ref[:, lo:hi]
        a = jnp.maximum(a, 0.0)
        a2 = (a * a).astype(BF16)
        part = jnp.dot(a2, wdown_ref[lo:hi, :], preferred_element_type=F32)
        acc = part if acc is None else acc + part
    return acc


def _layer0_kernel(prev_ref, cur_ref, wpool_ref, bpool_ref, pscale_ref, lng_ref, lnb_ref,
                   wup_ref, bup_ref, wdown_ref, bdown_ref, wkt_ref, wvt_ref, wq_ref,
                   h_out, kt_out, vt_out, qb_out, ktb_out, vtb_out, ext_ref, *, prompt, tm):
    i = pl.program_id(1)
    cur = cur_ref[0]
    if prompt:
        prev = jnp.where(i == 0, 0.0, prev_ref[0])
        pos = i * tm + lax.broadcasted_iota(jnp.int32, (tm, 1), 0)
    else:
        prev = jnp.zeros((HALO, D_MODEL), F32)
        pos = None
    ext_ref[0:HALO, :] = prev
    ext_ref[HALO:HALO + tm, :] = cur

    ys = []
    for g, w in enumerate(POOL_WINDOWS):
        c0, c1 = g * POOL_GROUP, (g + 1) * POOL_GROUP
        s = cur[:, c0:c1]
        for k in range(1, w):
            s = s + ext_ref[HALO - k:HALO - k + tm, c0:c1]
        if prompt:
            inv_cnt = 1.0 / jnp.minimum(pos + 1, w).astype(F32)
        else:
            inv_cnt = 1.0 / w
        p = s * inv_cnt - cur[:, c0:c1]
        ys.append(jnp.dot(p.astype(BF16), wpool_ref[g], preferred_element_type=F32))
    mix = (jnp.concatenate(ys, axis=-1) + bpool_ref[...]) * pscale_ref[...]

    h1 = _layer_norm(DEEPNORM_ALPHA * cur + mix, lng_ref[0:1, :], lnb_ref[0:1, :])
    m = _mlp(h1.astype(BF16), wup_ref, bup_ref, wdown_ref) + bdown_ref[...]
    h2 = _layer_norm(DEEPNORM_ALPHA * h1 + m, lng_ref[1:2, :], lnb_ref[1:2, :])
    h_out[0] = h2

    h2b = h2.astype(BF16)
    kt = lax.dot_general(wkt_ref[...], h2b, _NT_DIMS, preferred_element_type=F32)
    kt_out[0] = kt
    ktb_out[0, 0] = kt.astype(BF16)
    vt = lax.dot_general(wvt_ref[...], h2b, _NT_DIMS, preferred_element_type=F32)
    vt_out[0] = vt
    vtb_out[0, 0] = vt.astype(BF16)
    q = jnp.dot(h2b, wq_ref[...], preferred_element_type=F32)
    qb_out[0] = (q * (HEAD_DIM ** -0.5)).astype(BF16)


def _const_spec(shape):
    nd = len(shape)
    return pl.BlockSpec(shape, lambda *_: (0,) * nd, pipeline_mode=pl.Buffered(1))


def _layer0_call(u, n_rows_out, wts, *, prompt):
    nb, rows, _ = u.shape
    tm = ROW_TILE
    nt = rows // tm
    halo_blocks = tm // HALO
    row_spec = pl.BlockSpec((1, tm, D_MODEL), lambda b, i: (b, i, 0))
    prev_spec = pl.BlockSpec((1, HALO, D_MODEL), lambda b, i: (b, jnp.maximum(i * halo_blocks - 1, 0), 0))
    col_spec = pl.BlockSpec((1, D_MODEL, tm), lambda b, i: (b, 0, i))
    tile_spec = pl.BlockSpec((1, 1, D_MODEL, tm), lambda b, i: (b, i, 0, 0))
    in_specs = [prev_spec, row_spec,
                _const_spec((len(POOL_WINDOWS), POOL_GROUP, POOL_GROUP)),
                _const_spec((1, D_MODEL)), _const_spec((1, D_MODEL)),
                _const_spec((2, D_MODEL)), _const_spec((2, D_MODEL)),
                _const_spec((D_MODEL, D_FF)), _const_spec((1, D_FF)),
                _const_spec((D_FF, D_MODEL)), _const_spec((1, D_MODEL)),
                _const_spec((D_MODEL, D_MODEL)), _const_spec((D_MODEL, D_MODEL)),
                _const_spec((D_MODEL, D_MODEL))]
    out_shape = (jax.ShapeDtypeStruct((nb, rows, D_MODEL), F32),
                 jax.ShapeDtypeStruct((nb, D_MODEL, n_rows_out), F32),
                 jax.ShapeDtypeStruct((nb, D_MODEL, n_rows_out), F32),
                 jax.ShapeDtypeStruct((nb, rows, D_MODEL), BF16),
                 jax.ShapeDtypeStruct((nb, nt, D_MODEL, tm), BF16),
                 jax.ShapeDtypeStruct((nb, nt, D_MODEL, tm), BF16))
    return pl.pallas_call(
        functools.partial(_layer0_kernel, prompt=prompt, tm=tm),
        grid=(nb, nt),
        in_specs=in_specs,
        out_specs=(row_spec, col_spec, col_spec, row_spec, tile_spec, tile_spec),
        out_shape=out_shape,
        scratch_shapes=[pltpu.VMEM((HALO + tm, D_MODEL), F32)],
        compiler_params=pltpu.CompilerParams(dimension_semantics=("parallel", "arbitrary"),
                                             vmem_limit_bytes=VMEM_LIMIT_BYTES),
        name="layer0",
    )(u, u, *wts)


def _layer1_kernel(o_ref, h_ref, wo_ref, lng_ref, lnb_ref, wup_ref, bup_ref, wdown_ref, bdown_ref, y_out):
    h = h_ref[...]
    mix = jnp.dot(o_ref[...], wo_ref[...], preferred_element_type=F32)
    h1 = _layer_norm(DEEPNORM_ALPHA * h + mix, lng_ref[0:1, :], lnb_ref[0:1, :])
    m = _mlp(h1.astype(BF16), wup_ref, bup_ref, wdown_ref) + bdown_ref[...]
    y_out[...] = _layer_norm(DEEPNORM_ALPHA * h1 + m, lng_ref[1:2, :], lnb_ref[1:2, :])


def _layer1_call(o, h, wts, tm):
    rows = h.shape[0]
    row_spec = pl.BlockSpec((tm, D_MODEL), lambda i: (i, 0))
    in_specs = [row_spec, row_spec,
                _const_spec((D_MODEL, D_MODEL)),
                _const_spec((2, D_MODEL)), _const_spec((2, D_MODEL)),
                _const_spec((D_MODEL, D_FF)), _const_spec((1, D_FF)),
                _const_spec((D_FF, D_MODEL)), _const_spec((1, D_MODEL))]
    return pl.pallas_call(
        _layer1_kernel,
        grid=(rows // tm,),
        in_specs=in_specs,
        out_specs=row_spec,
        out_shape=jax.ShapeDtypeStruct((rows, D_MODEL), F32),
        compiler_params=pltpu.CompilerParams(dimension_semantics=("parallel",),
                                             vmem_limit_bytes=VMEM_LIMIT_BYTES),
        name="layer1",
    )(o, h, *wts)


SOFTPLUS_LINEAR_ABOVE = 80.0


def _softplus(z):
    return jnp.maximum(jnp.log(1.0 + jnp.exp(jnp.minimum(z, SOFTPLUS_LINEAR_ABOVE))), z)


def _suffix_sum(sp, tri):
    return jnp.dot(sp.astype(BF16), tri, preferred_element_type=F32)


def _attn_prompt_kernel(bias_ref, q_ref, k_ref, v_ref, tri_ref, *refs, tq, first_tile):
    o_ref, qh_buf, z_buf, a_buf, acc_ref, carry_ref = refs[-6:]
    g = pl.program_id(1)
    i = pl.program_id(2) + first_tile
    tk, nh = ATT_TK, ATT_HEADS
    lane = lax.broadcasted_iota(jnp.int32, (tq, LANES), 1)
    pair_sub = [divmod(h, 2) for h in range(nh)]
    for h, (pr, sub) in enumerate(pair_sub):
        qp = q_ref[0, :, pr * LANES:(pr + 1) * LANES]
        qh_buf[h] = jnp.where(lane // HEAD_DIM == sub, qp, jnp.zeros_like(qp))
    acc_ref[...] = jnp.zeros_like(acc_ref)
    carry_ref[...] = jnp.zeros_like(carry_ref)

    def key_tile(p):
        return jnp.maximum(i - p, 0)

    def qk(h, t):
        pr = pair_sub[h][0]
        kt = k_ref[0, t, pr * LANES:(pr + 1) * LANES, :]
        return jnp.dot(qh_buf[h], kt, preferred_element_type=F32)

    def av(h, t):
        pr = pair_sub[h][0]
        vt = v_ref[0, t, pr * LANES:(pr + 1) * LANES, :]
        return lax.dot_general(a_buf[h], vt, _NT_DIMS, preferred_element_type=F32)

    def weights(h, z, s, mask):
        c = carry_ref[h]
        suffix = s + jnp.concatenate([c] * (tk // LANES), axis=1)
        a = jnp.exp(z - suffix)
        if mask is not None:
            a = jnp.where(mask, a, 0.0)
        a_buf[h] = a.astype(BF16)
        carry_ref[h] = jnp.broadcast_to(suffix[:, 0:1], (tq, LANES))

    def sweep(p1, p3, do1, do2, do3, masked):
        tri = tri_ref[...]
        mask = None
        if masked:
            row = lax.broadcasted_iota(jnp.int32, (tq, tk), 0)
            col = lax.broadcasted_iota(jnp.int32, (tq, tk), 1)
            mask = col < row
        t1, t3 = key_tile(p1), key_tile(p3)
        zs, ss, avs, qks = {}, {}, {}, {}
        for k in range(nh + 2):
            if k < nh:
                if do3:
                    avs[k] = av(k, t3)
                if do2:
                    z = z_buf[k]
                    sp = _softplus(z)
                    if masked:
                        sp = jnp.where(mask, sp, 0.0)
                    zs[k], ss[k] = z, _suffix_sum(sp, tri)
            h = k - 1
            if 0 <= h < nh:
                if do3:
                    acc_ref[h] += avs.pop(h)
                if do2:
                    weights(h, zs.pop(h), ss.pop(h), mask)
                if do1:
                    qks[h] = qk(h, t1)
            h = k - 2
            if 0 <= h < nh and do1:
                z_buf[h] = qks.pop(h) + bias_ref[nh * g + h]

    sweep(0, 0, True, False, False, False)
    sweep(1, 0, True, True, False, True)

    def body(n, _):
        sweep(n, n - 2, True, True, True, False)
        return 0

    lax.fori_loop(2, i + 1, body, 0)

    @pl.when(i >= 1)
    def _():
        sweep(0, i - 1, False, True, True, False)

    sweep(0, i, False, False, True, False)
    if tq < ATT_TQ:
        o_ref[0, tq:, :] = jnp.zeros((ATT_TQ - tq, nh * HEAD_DIM), o_ref.dtype)
    for pr in range(nh // 2):
        o_ref[0, 0:tq, pr * LANES:(pr + 1) * LANES] = jnp.where(
            lane // HEAD_DIM == 0, acc_ref[2 * pr], acc_ref[2 * pr + 1]).astype(o_ref.dtype)


def _attn_prompt_call(bias, qb, ktb, vtb, tri, t_real):
    nb, rows, _ = qb.shape
    nk = ktb.shape[1]
    width = ATT_HEADS * HEAD_DIM
    n_full, tail = divmod(t_real, ATT_TQ)
    tq_tail = -(-tail // BF16_SUBLANES) * BF16_SUBLANES
    if tail == 0 or ATT_TQ % tq_tail != 0:
        n_full, tail = rows // ATT_TQ, 0
    kv_spec = pl.BlockSpec((1, nk, width, ATT_TK), lambda b, g, i: (b, 0, g, 0))
    tri_spec = pl.BlockSpec((ATT_TK, ATT_TK), lambda b, g, i: (0, 0))
    o_spec = lambda first: pl.BlockSpec((1, ATT_TQ, width), lambda b, g, i: (b, i + first, g))

    def call(tq, first_tile, n_tiles, q_spec, extra_in, extra_specs, aliases):
        return pl.pallas_call(
            functools.partial(_attn_prompt_kernel, tq=tq, first_tile=first_tile),
            grid=(nb, N_HEADS // ATT_HEADS, n_tiles),
            in_specs=[pl.BlockSpec(memory_space=pltpu.SMEM), q_spec, kv_spec, kv_spec, tri_spec] + extra_specs,
            out_specs=o_spec(first_tile),
            out_shape=jax.ShapeDtypeStruct((nb, rows, D_MODEL), BF16),
            scratch_shapes=[pltpu.VMEM((ATT_HEADS, tq, LANES), BF16),
                            pltpu.VMEM((ATT_HEADS, tq, ATT_TK), F32),
                            pltpu.VMEM((ATT_HEADS, tq, ATT_TK), BF16),
                            pltpu.VMEM((ATT_HEADS, tq, LANES), F32),
                            pltpu.VMEM((ATT_HEADS, tq, LANES), F32)],
            input_output_aliases=aliases,
            compiler_params=pltpu.CompilerParams(dimension_semantics=("parallel", "parallel", "arbitrary"),
                                                 vmem_limit_bytes=VMEM_LIMIT_BYTES),
            name="attn_prompt" if tq == ATT_TQ else "attn_prompt_tail",
        )(bias, qb, ktb, vtb, tri, *extra_in)

    o = call(ATT_TQ, 0, n_full, pl.BlockSpec((1, ATT_TQ, width), lambda b, g, i: (b, i, g)), [], [], {})
    if tail:
        per_tile = ATT_TQ // tq_tail
        q_tail = pl.BlockSpec((1, tq_tail, width), lambda b, g, i: (b, n_full * per_tile, g))
        o = call(tq_tail, n_full, 1, q_tail, [o], [pl.BlockSpec(memory_space=pl.ANY)], {5: 0})
    return o


def _attn_sample_kernel(pt_ref, qbd_ref, bias_ref, knew_ref, vnew_ref, *refs, n_new, npp):
    del pt_ref
    kpages, vpages = refs[:npp], refs[npp:2 * npp]
    tri_ref, selt_ref, fold_ref, o_ref, kcat, vcat, a_buf, acc_t, carry_ref = refs[2 * npp:]
    j = pl.program_id(1)
    nrow = N_HEADS * n_new
    chunk = ATT_TK
    n_chunks = npp * PAGE_SIZE // chunk
    qbd = qbd_ref[0]
    tri = tri_ref[...]

    @pl.when(j == 0)
    def _():
        qi = lax.broadcasted_iota(jnp.int32, (nrow, PAGE_SIZE), 0) // N_HEADS
        slot = lax.broadcasted_iota(jnp.int32, (nrow, PAGE_SIZE), 1)
        mask = slot < qi
        z = jnp.dot(qbd, knew_ref[0], preferred_element_type=F32) + bias_ref[:, 0:PAGE_SIZE]
        sp = jnp.where(mask, _softplus(z), 0.0)
        suffix = _suffix_sum(sp, tri[0:PAGE_SIZE, 0:PAGE_SIZE])
        a = jnp.where(mask, jnp.exp(z - suffix), 0.0).astype(BF16)
        acc_t[...] = lax.dot_general(vnew_ref[0], a, _NT_DIMS, preferred_element_type=F32)
        carry_ref[...] = jnp.broadcast_to(suffix[:, 0:1], carry_ref.shape)
        a_buf[...] = jnp.zeros_like(a_buf)

    for k in range(npp):
        vcat[:, k * PAGE_SIZE:(k + 1) * PAGE_SIZE] = vpages[k][0].astype(BF16)
    acc_t[...] += lax.dot_general(vcat[...], a_buf[...], _NT_DIMS, preferred_element_type=F32)

    for k in range(npp):
        kcat[:, k * PAGE_SIZE:(k + 1) * PAGE_SIZE] = kpages[k][0].astype(BF16)
    z = jnp.dot(qbd, kcat[...], preferred_element_type=F32) + bias_ref[...]
    sp = _softplus(z)
    carry = carry_ref[...]
    for c in reversed(range(n_chunks)):
        lanes = slice(c * chunk, (c + 1) * chunk)
        suffix = _suffix_sum(sp[:, lanes], tri) + jnp.concatenate([carry] * (chunk // LANES), axis=1)
        a_buf[:, lanes] = jnp.exp(z[:, lanes] - suffix).astype(BF16)
        carry = jnp.broadcast_to(suffix[:, 0:1], carry.shape)
    carry_ref[...] = carry

    @pl.when(j == pl.num_programs(1) - 1)
    def _():
        picked = (acc_t[...] * selt_ref[...]).astype(BF16)
        o = lax.dot_general(fold_ref[...], picked, _NT_DIMS, preferred_element_type=F32)
        o_ref[0] = o[0:n_new].astype(o_ref.dtype)


def _attn_sample_call(page_table, qbd, bias_rows, knew, vnew, cache_kt, cache_vt, tri, selt, fold, n_new):
    nseq, n_pages = page_table.shape
    npp = SAMPLE_PAGES
    nsteps = n_pages // npp
    nrow = N_HEADS * n_new

    def page_map(k, shift):
        def index_map(b, j, pt):
            step = jnp.clip(j - shift, 0, nsteps - 1)
            return (pt[b, n_pages - (step + 1) * npp + k], 0, 0)
        return index_map

    seq3 = lambda b, j, pt: (b, 0, 0)
    const2 = lambda b, j, pt: (0, 0)
    page_block = (1, D_MODEL, PAGE_SIZE)
    grid_spec = pltpu.PrefetchScalarGridSpec(
        num_scalar_prefetch=1,
        grid=(nseq, nsteps + 1),
        in_specs=([pl.BlockSpec((1, nrow, D_MODEL), seq3),
                   pl.BlockSpec((nrow, npp * PAGE_SIZE), const2),
                   pl.BlockSpec(page_block, seq3),
                   pl.BlockSpec(page_block, seq3)]
                  + [pl.BlockSpec(page_block, page_map(k, 0)) for k in range(npp)]
                  + [pl.BlockSpec(page_block, page_map(k, 1)) for k in range(npp)]
                  + [pl.BlockSpec((ATT_TK, ATT_TK), const2),
                     pl.BlockSpec((D_MODEL, nrow), const2),
                     pl.BlockSpec((HALO, nrow), const2)]),
        out_specs=pl.BlockSpec((1, n_new, D_MODEL), seq3),
        scratch_shapes=[pltpu.VMEM((D_MODEL, npp * PAGE_SIZE), BF16),
                        pltpu.VMEM((D_MODEL, npp * PAGE_SIZE), BF16),
                        pltpu.VMEM((nrow, npp * PAGE_SIZE), BF16),
                        pltpu.VMEM((D_MODEL, nrow), F32),
                        pltpu.VMEM((nrow, LANES), F32)])
    return pl.pallas_call(
        functools.partial(_attn_sample_kernel, n_new=n_new, npp=npp),
        grid_spec=grid_spec,
        out_shape=jax.ShapeDtypeStruct((nseq, n_new, D_MODEL), BF16),
        compiler_params=pltpu.CompilerParams(dimension_semantics=("parallel", "arbitrary"),
                                             vmem_limit_bytes=VMEM_LIMIT_BYTES),
        name="attn_sample",
    )(page_table, qbd, bias_rows, knew, vnew, *([cache_kt] * npp), *([cache_vt] * npp), tri, selt, fold)


def _suffix_matrix(n):
    j = jnp.arange(n)[:, None]
    s = jnp.arange(n)[None, :]
    return (j >= s).astype(BF16)


def kernel(x_prompt, x_sample, cache_k, cache_v, state_pool, page_table, meta_tokens, ln_g, ln_b, w_pool, b_pool,
           pool_scale, w_up, b_up, w_down, b_down, w_kv, w_q, w_o, sb_bias):
    nb, seq, _ = x_prompt.shape
    nseq, n_new, _ = x_sample.shape
    t_real = seq + N_META
    t_pad = -(-t_real // ATT_TK) * ATT_TK
    assert n_new <= HALO and page_table.shape[1] % SAMPLE_PAGES == 0

    wts0 = (w_pool[0].astype(BF16), b_pool[0:1], pool_scale[0:1], ln_g[0], ln_b[0],
            w_up[0].astype(BF16), b_up[0:1], w_down[0].astype(BF16), b_down[0:1],
            w_kv[:, :D_MODEL].T.astype(BF16), w_kv[:, D_MODEL:].T.astype(BF16), w_q[0].astype(BF16))
    wts1 = (w_o[0].astype(BF16), ln_g[1], ln_b[1], w_up[1].astype(BF16), b_up[1:2],
            w_down[1].astype(BF16), b_down[1:2])
    bias = sb_bias[0].astype(F32)
    tri = _suffix_matrix(ATT_TK)

    h0 = jnp.concatenate([jnp.broadcast_to(meta_tokens[None], (nb, N_META, D_MODEL)), x_prompt,
                          jnp.zeros((nb, t_pad - t_real, D_MODEL), F32)], axis=1)
    h_p, kt_p, vt_p, qb_p, ktb_p, vtb_p = _layer0_call(h0, t_real, wts0, prompt=True)
    o_p = _attn_prompt_call(bias, qb_p, ktb_p, vtb_p, tri, t_real)
    y_p = _layer1_call(o_p.reshape(nb * t_pad, D_MODEL), h_p.reshape(nb * t_pad, D_MODEL), wts1, ROW_TILE)
    y_prompt = y_p.reshape(nb, t_pad, D_MODEL)[:, N_META:t_real]
    pool_prompt = x_prompt[None, :, seq - POOL_CTX:, :]

    slot = 2 * HALO
    ctx = state_pool[0]
    u_s = jnp.concatenate([jnp.zeros((nseq, HALO - POOL_CTX, D_MODEL), F32), ctx, x_sample,
                           jnp.zeros((nseq, HALO - n_new, D_MODEL), F32)], axis=1)
    h_s, kt_s, vt_s, qb_s, _, _ = _layer0_call(u_s.reshape(1, nseq * slot, D_MODEL), nseq * slot, wts0, prompt=False)
    take = lambda a: a.reshape(nseq, slot, D_MODEL)[:, HALO:HALO + n_new]
    h_s, qb_s = take(h_s), take(qb_s)
    take_t = lambda a: a.reshape(D_MODEL, nseq, slot)[:, :, HALO:HALO + n_new].transpose(1, 0, 2)
    kt_s, vt_s = take_t(kt_s), take_t(vt_s)

    nrow = N_HEADS * n_new
    head_of_lane = jnp.arange(D_MODEL) // HEAD_DIM
    sel = head_of_lane[None, :] == (jnp.arange(nrow) % N_HEADS)[:, None]
    qbd = jnp.where(sel[None], jnp.repeat(qb_s, N_HEADS, axis=1), jnp.zeros((), BF16))
    bias_rows = jnp.broadcast_to(jnp.tile(bias, n_new)[:, None], (nrow, SAMPLE_PAGES * PAGE_SIZE))
    fold = (jnp.arange(HALO)[:, None] == (jnp.arange(nrow) // N_HEADS)[None, :]).astype(BF16)
    pad_new = lambda a: jnp.pad(a.astype(BF16), ((0, 0), (0, 0), (0, PAGE_SIZE - n_new)))
    n_phys = cache_k.shape[0]
    pages_t = lambda c: c.transpose(0, 2, 3, 1).reshape(n_phys, D_MODEL, PAGE_SIZE)
    o_s = _attn_sample_call(page_table, qbd, bias_rows, pad_new(kt_s), pad_new(vt_s),
                            pages_t(cache_k), pages_t(cache_v), tri, sel.T.astype(F32), fold, n_new)
    y_s = _layer1_call(o_s.reshape(nseq * n_new, D_MODEL), h_s.reshape(nseq * n_new, D_MODEL), wts1, nseq * n_new)
    y_sample = y_s.reshape(nseq, n_new, D_MODEL)
    pool_sample = jnp.concatenate([ctx, x_sample], axis=1)[None, :, n_new:, :]

    rows = lambda a: a.reshape(a.shape[0], N_HEADS, HEAD_DIM, a.shape[2]).transpose(0, 3, 1, 2)
    return (y_prompt, y_sample, rows(kt_p), rows(vt_p), rows(kt_s), rows(vt_s), pool_prompt, pool_sample)
```

```python
import functools

import jax
import jax.numpy as jnp
from jax import lax
from jax.experimental import pallas as pl
from jax.experimental.pallas import tpu as pltpu

D_MODEL = 1024
N_META = 16
POOL_WINDOWS = (2, 4, 8, 16)
POOL_GROUP = D_MODEL // len(POOL_WINDOWS)
POOL_CTX = max(POOL_WINDOWS) - 1
HALO = 16
HEAD_DIM = 64
N_HEADS = D_MODEL // HEAD_DIM
D_FF = 4 * D_MODEL
LN_EPS = 1e-5
DEPTH = 2
DEEPNORM_ALPHA = (2.0 * DEPTH) ** 0.25
PAGE_SIZE = 128

ROW_TILE = 256
ATT_TQ = 256
ATT_TK = 256
ATT_HEADS = 16
SAMPLE_PAGES = 16
LANES = 128
BF16_SUBLANES = 16
VMEM_LIMIT_BYTES = 56 * 1024 * 1024

F32 = jnp.float32
BF16 = jnp.bfloat16
_NT_DIMS = (((1,), (1,)), ((), ()))


def _layer_norm(x, g, b):
    mu = jnp.mean(x, axis=-1, keepdims=True)
    xc = x - mu
    var = jnp.mean(xc * xc, axis=-1, keepdims=True)
    return xc * lax.rsqrt(var + LN_EPS) * g + b


def _mlp(hb, wup_ref, bup_ref, wdown_ref):
    acc = None
    for c in range(D_FF // D_MODEL):
        lo, hi = c * D_MODEL, (c + 1) * D_MODEL
        a = jnp.dot(hb, wup_ref[:, lo:hi], preferred_element_type=F32) + bup_ref[:, lo:hi]
        a = jnp.maximum(a, 0.0)
        a2 = (a * a).astype(BF16)
        part = jnp.dot(a2, wdown_ref[lo:hi, :], preferred_element_type=F32)
        acc = part if acc is None else acc + part
    return acc


def _layer0_kernel(prev_ref, cur_ref, wpool_ref, bpool_ref, pscale_ref, lng_ref, lnb_ref,
                   wup_ref, bup_ref, wdown_ref, bdown_ref, wkt_ref, wvt_ref, wq_ref,
                   h_out, kt_out, vt_out, qb_out, ktb_out, vtb_out, ext_ref, *, prompt, tm):
    i = pl.program_id(1)
    cur = cur_ref[0]
    if prompt:
        prev = jnp.where(i == 0, 0.0, prev_ref[0])
        pos = i * tm + lax.broadcasted_iota(jnp.int32, (tm, 1), 0)
    else:
        prev = jnp.zeros((HALO, D_MODEL), F32)
        pos = None
    ext_ref[0:HALO, :] = prev
    ext_ref[HALO:HALO + tm, :] = cur

    ys = []
    for g, w in enumerate(POOL_WINDOWS):
        c0, c1 = g * POOL_GROUP, (g + 1) * POOL_GROUP
        s = cur[:, c0:c1]
        for k in range(1, w):
            s = s + ext_ref[HALO - k:HALO - k + tm, c0:c1]
        if prompt:
            inv_cnt = 1.0 / jnp.minimum(pos + 1, w).astype(F32)
        else:
            inv_cnt = 1.0 / w
        p = s * inv_cnt - cur[:, c0:c1]
        ys.append(jnp.dot(p.astype(BF16), wpool_ref[g], preferred_element_type=F32))
    mix = (jnp.concatenate(ys, axis=-1) + bpool_ref[...]) * pscale_ref[...]

    h1 = _layer_norm(DEEPNORM_ALPHA * cur + mix, lng_ref[0:1, :], lnb_ref[0:1, :])
    m = _mlp(h1.astype(BF16), wup_ref, bup_ref, wdown_ref) + bdown_ref[...]
    h2 = _layer_norm(DEEPNORM_ALPHA * h1 + m, lng_ref[1:2, :], lnb_ref[1:2, :])
    h_out[0] = h2

    h2b = h2.astype(BF16)
    kt = lax.dot_general(wkt_ref[...], h2b, _NT_DIMS, preferred_element_type=F32)
    kt_out[0] = kt
    ktb_out[0, 0] = kt.astype(BF16)
    vt = lax.dot_general(wvt_ref[...], h2b, _NT_DIMS, preferred_element_type=F32)
    vt_out[0] = vt
    vtb_out[0, 0] = vt.astype(BF16)
    q = jnp.dot(h2b, wq_ref[...], preferred_element_type=F32)
    qb_out[0] = (q * (HEAD_DIM ** -0.5)).astype(BF16)


def _const_spec(shape):
    nd = len(shape)
    return pl.BlockSpec(shape, lambda *_: (0,) * nd, pipeline_mode=pl.Buffered(1))


def _layer0_call(u, n_rows_out, wts, *, prompt):
    nb, rows, _ = u.shape
    tm = ROW_TILE
    nt = rows // tm
    halo_blocks = tm // HALO
    row_spec = pl.BlockSpec((1, tm, D_MODEL), lambda b, i: (b, i, 0))
    prev_spec = pl.BlockSpec((1, HALO, D_MODEL), lambda b, i: (b, jnp.maximum(i * halo_blocks - 1, 0), 0))
    col_spec = pl.BlockSpec((1, D_MODEL, tm), lambda b, i: (b, 0, i))
    tile_spec = pl.BlockSpec((1, 1, D_MODEL, tm), lambda b, i: (b, i, 0, 0))
    in_specs = [prev_spec, row_spec,
                _const_spec((len(POOL_WINDOWS), POOL_GROUP, POOL_GROUP)),
                _const_spec((1, D_MODEL)), _const_spec((1, D_MODEL)),
                _const_spec((2, D_MODEL)), _const_spec((2, D_MODEL)),
                _const_spec((D_MODEL, D_FF)), _const_spec((1, D_FF)),
                _const_spec((D_FF, D_MODEL)), _const_spec((1, D_MODEL)),
                _const_spec((D_MODEL, D_MODEL)), _const_spec((D_MODEL, D_MODEL)),
                _const_spec((D_MODEL, D_MODEL))]
    out_shape = (jax.ShapeDtypeStruct((nb, rows, D_MODEL), F32),
                 jax.ShapeDtypeStruct((nb, D_MODEL, n_rows_out), F32),
                 jax.ShapeDtypeStruct((nb, D_MODEL, n_rows_out), F32),
                 jax.ShapeDtypeStruct((nb, rows, D_MODEL), BF16),
                 jax.ShapeDtypeStruct((nb, nt, D_MODEL, tm), BF16),
                 jax.ShapeDtypeStruct((nb, nt, D_MODEL, tm), BF16))
    return pl.pallas_call(
        functools.partial(_layer0_kernel, prompt=prompt, tm=tm),
        grid=(nb, nt),
        in_specs=in_specs,
        out_specs=(row_spec, col_spec, col_spec, row_spec, tile_spec, tile_spec),
        out_shape=out_shape,
        scratch_shapes=[pltpu.VMEM((HALO + tm, D_MODEL), F32)],
        compiler_params=pltpu.CompilerParams(dimension_semantics=("parallel", "arbitrary"),
                                             vmem_limit_bytes=VMEM_LIMIT_BYTES),
        name="layer0",
    )(u, u, *wts)


def _layer1_kernel(o_ref, h_ref, wo_ref, lng_ref, lnb_ref, wup_ref, bup_ref, wdown_ref, bdown_ref, y_out):
    h = h_ref[...]
    mix = jnp.dot(o_ref[...], wo_ref[...], preferred_element_type=F32)
    h1 = _layer_norm(DEEPNORM_ALPHA * h + mix, lng_ref[0:1, :], lnb_ref[0:1, :])
    m = _mlp(h1.astype(BF16), wup_ref, bup_ref, wdown_ref) + bdown_ref[...]
    y_out[...] = _layer_norm(DEEPNORM_ALPHA * h1 + m, lng_ref[1:2, :], lnb_ref[1:2, :])


def _layer1_call(o, h, wts, tm):
    rows = h.shape[0]
    row_spec = pl.BlockSpec((tm, D_MODEL), lambda i: (i, 0))
    in_specs = [row_spec, row_spec,
                _const_spec((D_MODEL, D_MODEL)),
                _const_spec((2, D_MODEL)), _const_spec((2, D_MODEL)),
                _const_spec((D_MODEL, D_FF)), _const_spec((1, D_FF)),
                _const_spec((D_FF, D_MODEL)), _const_spec((1, D_MODEL))]
    return pl.pallas_call(
        _layer1_kernel,
        grid=(rows // tm,),
        in_specs=in_specs,
        out_specs=row_spec,
        out_shape=jax.ShapeDtypeStruct((rows, D_MODEL), F32),
        compiler_params=pltpu.CompilerParams(dimension_semantics=("parallel",),
                                             vmem_limit_bytes=VMEM_LIMIT_BYTES),
        name="layer1",
    )(o, h, *wts)


SOFTPLUS_LINEAR_ABOVE = 80.0


def _softplus(z):
    return jnp.maximum(jnp.log(1.0 + jnp.exp(jnp.minimum(z, SOFTPLUS_LINEAR_ABOVE))), z)


def _suffix_sum(sp, tri):
    return jnp.dot(sp.astype(BF16), tri, preferred_element_type=F32)


def _attn_prompt_kernel(bias_ref, q_ref, k_ref, v_ref, tri_ref, *refs, tq, first_tile):
    o_ref, qh_buf, z_buf, a_buf, acc_ref, carry_ref = refs[-6:]
    g = pl.program_id(1)
    i = pl.program_id(2) + first_tile
    tk, nh = ATT_TK, ATT_HEADS
    lane = lax.broadcasted_iota(jnp.int32, (tq, LANES), 1)
    pair_sub = [divmod(h, 2) for h in range(nh)]
    for h, (pr, sub) in enumerate(pair_sub):
        qp = q_ref[0, :, pr * LANES:(pr + 1) * LANES]
        qh_buf[h] = jnp.where(lane // HEAD_DIM == sub, qp, jnp.zeros_like(qp))
    acc_ref[...] = jnp.zeros_like(acc_ref)
    carry_ref[...] = jnp.zeros_like(carry_ref)

    def key_tile(p):
        return jnp.maximum(i - p, 0)

    def qk(h, t):
        pr = pair_sub[h][0]
        kt = k_ref[0, t, pr * LANES:(pr + 1) * LANES, :]
        return jnp.dot(qh_buf[h], kt, preferred_element_type=F32)

    def av(h, t):
        pr = pair_sub[h][0]
        vt = v_ref[0, t, pr * LANES:(pr + 1) * LANES, :]
        return lax.dot_general(a_buf[h], vt, _NT_DIMS, preferred_element_type=F32)

    def weights(h, z, s, mask):
        c = carry_ref[h]
        suffix = s + jnp.concatenate([c] * (tk // LANES), axis=1)
        a = jnp.exp(z - suffix)
        if mask is not None:
            a = jnp.where(mask, a, 0.0)
        a_buf[h] = a.astype(BF16)
        carry_ref[h] = jnp.broadcast_to(suffix[:, 0:1], (tq, LANES))

    def sweep(p1, p3, do1, do2, do3, masked):
        tri = tri_ref[...]
        mask = None
        if masked:
            row = lax.broadcasted_iota(jnp.int32, (tq, tk), 0)
            col = lax.broadcasted_iota(jnp.int32, (tq, tk), 1)
            mask = col < row
        t1, t3 = key_tile(p1), key_tile(p3)
        zs, ss, avs, qks = {}, {}, {}, {}
        for k in range(nh + 2):
            if k < nh:
                if do3:
                    avs[k] = av(k, t3)
                if do2:
                    z = z_buf[k]
                    sp = _softplus(z)
                    if masked:
                        sp = jnp.where(mask, sp, 0.0)
                    zs[k], ss[k] = z, _suffix_sum(sp, tri)
            h = k - 1
            if 0 <= h < nh:
                if do3:
                    acc_ref[h] += avs.pop(h)
                if do2:
                    weights(h, zs.pop(h), ss.pop(h), mask)
                if do1:
                    qks[h] = qk(h, t1)
            h = k - 2
            if 0 <= h < nh and do1:
                z_buf[h] = qks.pop(h) + bias_ref[nh * g + h]

    sweep(0, 0, True, False, False, False)
    sweep(1, 0, True, True, False, True)

    def body(n, _):
        sweep(n, n - 2, True, True, True, False)
        return 0

    lax.fori_loop(2, i + 1, body, 0)

    @pl.when(i >= 1)
    def _():
        sweep(0, i - 1, False, True, True, False)

    sweep(0, i, False, False, True, False)
    if tq < ATT_TQ:
        o_ref[0, tq:, :] = jnp.zeros((ATT_TQ - tq, nh * HEAD_DIM), o_ref.dtype)
    for pr in range(nh // 2):
        o_ref[0, 0:tq, pr * LANES:(pr + 1) * LANES] = jnp.where(
            lane // HEAD_DIM == 0, acc_ref[2 * pr], acc_ref[2 * pr + 1]).astype(o_ref.dtype)


def _attn_prompt_call(bias, qb, ktb, vtb, tri, t_real):
    nb, rows, _ = qb.shape
    nk = ktb.shape[1]
    width = ATT_HEADS * HEAD_DIM
    n_full, tail = divmod(t_real, ATT_TQ)
    tq_tail = -(-tail // BF16_SUBLANES) * BF16_SUBLANES
    if tail == 0 or ATT_TQ % tq_tail != 0:
        n_full, tail = rows // ATT_TQ, 0
    kv_spec = pl.BlockSpec((1, nk, width, ATT_TK), lambda b, g, i: (b, 0, g, 0), pipeline_mode=pl.Buffered(1))
    tri_spec = pl.BlockSpec((ATT_TK, ATT_TK), lambda b, g, i: (0, 0))
    o_spec = lambda first: pl.BlockSpec((1, ATT_TQ, width), lambda b, g, i: (b, i + first, g))

    def call(tq, first_tile, n_tiles, q_spec, extra_in, extra_specs, aliases):
        return pl.pallas_call(
            functools.partial(_attn_prompt_kernel, tq=tq, first_tile=first_tile),
            grid=(nb, N_HEADS // ATT_HEADS, n_tiles),
            in_specs=[pl.BlockSpec(memory_space=pltpu.SMEM), q_spec, kv_spec, kv_spec, tri_spec] + extra_specs,
            out_specs=o_spec(first_tile),
            out_shape=jax.ShapeDtypeStruct((nb, rows, D_MODEL), BF16),
            scratch_shapes=[pltpu.VMEM((ATT_HEADS, tq, LANES), BF16),
                            pltpu.VMEM((ATT_HEADS, tq, ATT_TK), F32),
                            pltpu.VMEM((ATT_HEADS, tq, ATT_TK), BF16),
                            pltpu.VMEM((ATT_HEADS, tq, LANES), F32),
                            pltpu.VMEM((ATT_HEADS, tq, LANES), F32)],
            input_output_aliases=aliases,
            compiler_params=pltpu.CompilerParams(dimension_semantics=("parallel", "parallel", "arbitrary"),
                                                 vmem_limit_bytes=VMEM_LIMIT_BYTES),
            name="attn_prompt" if tq == ATT_TQ else "attn_prompt_tail",
        )(bias, qb, ktb, vtb, tri, *extra_in)

    o = call(ATT_TQ, 0, n_full, pl.BlockSpec((1, ATT_TQ, width), lambda b, g, i: (b, i, g)), [], [], {})
    if tail:
        per_tile = ATT_TQ // tq_tail
        q_tail = pl.BlockSpec((1, tq_tail, width), lambda b, g, i: (b, n_full * per_tile, g))
        o = call(tq_tail, n_full, 1, q_tail, [o], [pl.BlockSpec(memory_space=pl.ANY)], {5: 0})
    return o


def _attn_sample_kernel(pt_ref, qbd_ref, bias_ref, knew_ref, vnew_ref, *refs, n_new, npp):
    del pt_ref
    kpages, vpages = refs[:npp], refs[npp:2 * npp]
    tri_ref, selt_ref, fold_ref, o_ref, kcat, vcat, a_buf, acc_t, carry_ref = refs[2 * npp:]
    j = pl.program_id(1)
    nrow = N_HEADS * n_new
    chunk = ATT_TK
    n_chunks = npp * PAGE_SIZE // chunk
    qbd = qbd_ref[0]
    tri = tri_ref[...]

    @pl.when(j == 0)
    def _():
        qi = lax.broadcasted_iota(jnp.int32, (nrow, PAGE_SIZE), 0) // N_HEADS
        slot = lax.broadcasted_iota(jnp.int32, (nrow, PAGE_SIZE), 1)
        mask = slot < qi
        z = jnp.dot(qbd, knew_ref[0], preferred_element_type=F32) + bias_ref[:, 0:PAGE_SIZE]
        sp = jnp.where(mask, _softplus(z), 0.0)
        suffix = _suffix_sum(sp, tri[0:PAGE_SIZE, 0:PAGE_SIZE])
        a = jnp.where(mask, jnp.exp(z - suffix), 0.0).astype(BF16)
        acc_t[...] = lax.dot_general(vnew_ref[0], a, _NT_DIMS, preferred_element_type=F32)
        carry_ref[...] = jnp.broadcast_to(suffix[:, 0:1], carry_ref.shape)
        a_buf[...] = jnp.zeros_like(a_buf)

    for k in range(npp):
        vcat[:, k * PAGE_SIZE:(k + 1) * PAGE_SIZE] = vpages[k][0].astype(BF16)
    acc_t[...] += lax.dot_general(vcat[...], a_buf[...], _NT_DIMS, preferred_element_type=F32)

    for k in range(npp):
        kcat[:, k * PAGE_SIZE:(k + 1) * PAGE_SIZE] = kpages[k][0].astype(BF16)
    z = jnp.dot(qbd, kcat[...], preferred_element_type=F32) + bias_ref[...]
    sp = _softplus(z)
    carry = carry_ref[...]
    for c in reversed(range(n_chunks)):
        lanes = slice(c * chunk, (c + 1) * chunk)
        suffix = _suffix_sum(sp[:, lanes], tri) + jnp.concatenate([carry] * (chunk // LANES), axis=1)
        a_buf[:, lanes] = jnp.exp(z[:, lanes] - suffix).astype(BF16)
        carry = jnp.broadcast_to(suffix[:, 0:1], carry.shape)
    carry_ref[...] = carry

    @pl.when(j == pl.num_programs(1) - 1)
    def _():
        picked = (acc_t[...] * selt_ref[...]).astype(BF16)
        o = lax.dot_general(fold_ref[...], picked, _NT_DIMS, preferred_element_type=F32)
        o_ref[0] = o[0:n_new].astype(o_ref.dtype)


def _attn_sample_call(page_table, qbd, bias_rows, knew, vnew, cache_kt, cache_vt, tri, selt, fold, n_new):
    nseq, n_pages = page_table.shape
    npp = SAMPLE_PAGES
    nsteps = n_pages // npp
    nrow = N_HEADS * n_new

    def page_map(k, shift):
        def index_map(b, j, pt):
            step = jnp.clip(j - shift, 0, nsteps - 1)
            return (pt[b, n_pages - (step + 1) * npp + k], 0, 0)
        return index_map

    seq3 = lambda b, j, pt: (b, 0, 0)
    const2 = lambda b, j, pt: (0, 0)
    page_block = (1, D_MODEL, PAGE_SIZE)
    grid_spec = pltpu.PrefetchScalarGridSpec(
        num_scalar_prefetch=1,
        grid=(nseq, nsteps + 1),
        in_specs=([pl.BlockSpec((1, nrow, D_MODEL), seq3),
                   pl.BlockSpec((nrow, npp * PAGE_SIZE), const2),
                   pl.BlockSpec(page_block, seq3),
                   pl.BlockSpec(page_block, seq3)]
                  + [pl.BlockSpec(page_block, page_map(k, 0)) for k in range(npp)]
                  + [pl.BlockSpec(page_block, page_map(k, 1)) for k in range(npp)]
                  + [pl.BlockSpec((ATT_TK, ATT_TK), const2),
                     pl.BlockSpec((D_MODEL, nrow), const2),
                     pl.BlockSpec((HALO, nrow), const2)]),
        out_specs=pl.BlockSpec((1, n_new, D_MODEL), seq3),
        scratch_shapes=[pltpu.VMEM((D_MODEL, npp * PAGE_SIZE), BF16),
                        pltpu.VMEM((D_MODEL, npp * PAGE_SIZE), BF16),
                        pltpu.VMEM((nrow, npp * PAGE_SIZE), BF16),
                        pltpu.VMEM((D_MODEL, nrow), F32),
                        pltpu.VMEM((nrow, LANES), F32)])
    return pl.pallas_call(
        functools.partial(_attn_sample_kernel, n_new=n_new, npp=npp),
        grid_spec=grid_spec,
        out_shape=jax.ShapeDtypeStruct((nseq, n_new, D_MODEL), BF16),
        compiler_params=pltpu.CompilerParams(dimension_semantics=("parallel", "arbitrary"),
                                             vmem_limit_bytes=VMEM_LIMIT_BYTES),
        name="attn_sample",
    )(page_table, qbd, bias_rows, knew, vnew, *([cache_kt] * npp), *([cache_vt] * npp), tri, selt, fold)


def _suffix_matrix(n):
    j = jnp.arange(n)[:, None]
    s = jnp.arange(n)[None, :]
    return (j >= s).astype(BF16)


def kernel(x_prompt, x_sample, cache_k, cache_v, state_pool, page_table, meta_tokens, ln_g, ln_b, w_pool, b_pool,
           pool_scale, w_up, b_up, w_down, b_down, w_kv, w_q, w_o, sb_bias):
    nb, seq, _ = x_prompt.shape
    nseq, n_new, _ = x_sample.shape
    t_real = seq + N_META
    t_pad = -(-t_real // ATT_TK) * ATT_TK
    assert n_new <= HALO and page_table.shape[1] % SAMPLE_PAGES == 0

    wts0 = (w_pool[0].astype(BF16), b_pool[0:1], pool_scale[0:1], ln_g[0], ln_b[0],
            w_up[0].astype(BF16), b_up[0:1], w_down[0].astype(BF16), b_down[0:1],
            w_kv[:, :D_MODEL].T.astype(BF16), w_kv[:, D_MODEL:].T.astype(BF16), w_q[0].astype(BF16))
    wts1 = (w_o[0].astype(BF16), ln_g[1], ln_b[1], w_up[1].astype(BF16), b_up[1:2],
            w_down[1].astype(BF16), b_down[1:2])
    bias = sb_bias[0].astype(F32)
    tri = _suffix_matrix(ATT_TK)

    h0 = jnp.concatenate([jnp.broadcast_to(meta_tokens[None], (nb, N_META, D_MODEL)), x_prompt,
                          jnp.zeros((nb, t_pad - t_real, D_MODEL), F32)], axis=1)
    h_p, kt_p, vt_p, qb_p, ktb_p, vtb_p = _layer0_call(h0, t_real, wts0, prompt=True)
    o_p = _attn_prompt_call(bias, qb_p, ktb_p, vtb_p, tri, t_real)
    y_p = _layer1_call(o_p.reshape(nb * t_pad, D_MODEL), h_p.reshape(nb * t_pad, D_MODEL), wts1, ROW_TILE)
    y_prompt = y_p.reshape(nb, t_pad, D_MODEL)[:, N_META:t_real]
    pool_prompt = x_prompt[None, :, seq - POOL_CTX:, :]

    slot = 2 * HALO
    ctx = state_pool[0]
    u_s = jnp.concatenate([jnp.zeros((nseq, HALO - POOL_CTX, D_MODEL), F32), ctx, x_sample,
                           jnp.zeros((nseq, HALO - n_new, D_MODEL), F32)], axis=1)
    h_s, kt_s, vt_s, qb_s, _, _ = _layer0_call(u_s.reshape(1, nseq * slot, D_MODEL), nseq * slot, wts0, prompt=False)
    take = lambda a: a.reshape(nseq, slot, D_MODEL)[:, HALO:HALO + n_new]
    h_s, qb_s = take(h_s), take(qb_s)
    take_t = lambda a: a.reshape(D_MODEL, nseq, slot)[:, :, HALO:HALO + n_new].transpose(1, 0, 2)
    kt_s, vt_s = take_t(kt_s), take_t(vt_s)

    nrow = N_HEADS * n_new
    head_of_lane = jnp.arange(D_MODEL) // HEAD_DIM
    sel = head_of_lane[None, :] == (jnp.arange(nrow) % N_HEADS)[:, None]
    qbd = jnp.where(sel[None], jnp.repeat(qb_s, N_HEADS, axis=1), jnp.zeros((), BF16))
    bias_rows = jnp.broadcast_to(jnp.tile(bias, n_new)[:, None], (nrow, SAMPLE_PAGES * PAGE_SIZE))
    fold = (jnp.arange(HALO)[:, None] == (jnp.arange(nrow) // N_HEADS)[None, :]).astype(BF16)
    pad_new = lambda a: jnp.pad(a.astype(BF16), ((0, 0), (0, 0), (0, PAGE_SIZE - n_new)))
    n_phys = cache_k.shape[0]
    pages_t = lambda c: c.transpose(0, 2, 3, 1).reshape(n_phys, D_MODEL, PAGE_SIZE)
    o_s = _attn_sample_call(page_table, qbd, bias_rows, pad_new(kt_s), pad_new(vt_s),
                            pages_t(cache_k), pages_t(cache_v), tri, sel.T.astype(F32), fold, n_new)
    y_s = _layer1_call(o_s.reshape(nseq * n_new, D_MODEL), h_s.reshape(nseq * n_new, D_MODEL), wts1, nseq * n_new)
    y_sample = y_s.reshape(nseq, n_new, D_MODEL)
    pool_sample = jnp.concatenate([ctx, x_sample], axis=1)[None, :, n_new:, :]

    rows = lambda a: a.reshape(a.shape[0], N_HEADS, HEAD_DIM, a.shape[2]).transpose(0, 3, 1, 2)
    return (y_prompt, y_sample, rows(kt_p), rows(vt_p), rows(kt_s), rows(vt_s), pool_prompt, pool_sample)
```

```python
import functools

import jax
import jax.numpy as jnp
from jax import lax
from jax.experimental import pallas as pl
from jax.experimental.pallas import tpu as pltpu

D_MODEL = 1024
N_META = 16
POOL_WINDOWS = (2, 4, 8, 16)
POOL_GROUP = D_MODEL // len(POOL_WINDOWS)
POOL_CTX = max(POOL_WINDOWS) - 1
HALO = 16
HEAD_DIM = 64
N_HEADS = D_MODEL // HEAD_DIM
D_FF = 4 * D_MODEL
LN_EPS = 1e-5
DEPTH = 2
DEEPNORM_ALPHA = (2.0 * DEPTH) ** 0.25
PAGE_SIZE = 128

ROW_TILE = 256
ATT_TQ = 256
ATT_TK = 256
ATT_HEADS = 16
SAMPLE_PAGES = 16
LANES = 128
BF16_SUBLANES = 16
F32_SUBLANES = 8
VMEM_LIMIT_BYTES = 56 * 1024 * 1024

F32 = jnp.float32
BF16 = jnp.bfloat16
_NT_DIMS = (((1,), (1,)), ((), ()))


def _layer_norm(x, g, b):
    mu = jnp.mean(x, axis=-1, keepdims=True)
    xc = x - mu
    var = jnp.mean(xc * xc, axis=-1, keepdims=True)
    return xc * lax.rsqrt(var + LN_EPS) * g + b


def _mlp(hb, wup_ref, bup_ref, wdown_ref):
    acc = None
    for c in range(D_FF // D_MODEL):
        lo, hi = c * D_MODEL, (c + 1) * D_MODEL
        a = jnp.dot(hb, wup_ref[:, lo:hi], preferred_element_type=F32) + bup_ref[:, lo:hi]
        a = jnp.maximum(a, 0.0)
        a2 = (a * a).astype(BF16)
        part = jnp.dot(a2, wdown_ref[lo:hi, :], preferred_element_type=F32)
        acc = part if acc is None else acc + part
    return acc


def _layer0_kernel(prev_ref, cur_ref, meta_ref, wpool_ref, bpool_ref, pscale_ref, lng_ref, lnb_ref,
                   wup_ref, bup_ref, wdown_ref, bdown_ref, wkt_ref, wvt_ref, wq_ref,
                   h_out, kt_out, vt_out, qb_out, ktb_out, vtb_out, ext_ref, *, prompt, tm, stride):
    i = pl.program_id(1)
    halo = HALO * stride
    cur = cur_ref[0]
    if prompt:
        body = tm - N_META
        first_tile = jnp.concatenate([meta_ref[...], cur[0:body]], axis=0)
        last_tile = jnp.concatenate([cur[body:tm], jnp.zeros((body, D_MODEL), F32)], axis=0)
        cur = jnp.where(i == 0, first_tile, jnp.where(i == pl.num_programs(1) - 1, last_tile, cur))
        prev = jnp.where(i == 0, 0.0, prev_ref[0])
        pos = i * tm + lax.broadcasted_iota(jnp.int32, (tm, 1), 0)
    else:
        prev = prev_ref[0]
    ext_ref[0:halo, :] = prev
    ext_ref[halo:halo + tm, :] = cur

    ys = []
    for g, w in enumerate(POOL_WINDOWS):
        c0, c1 = g * POOL_GROUP, (g + 1) * POOL_GROUP
        s = cur[:, c0:c1]
        for k in range(1, w):
            s = s + ext_ref[halo - k * stride:halo - k * stride + tm, c0:c1]
        if prompt:
            inv_cnt = 1.0 / jnp.minimum(pos + 1, w).astype(F32)
        else:
            inv_cnt = 1.0 / w
        p = s * inv_cnt - cur[:, c0:c1]
        ys.append(jnp.dot(p.astype(BF16), wpool_ref[g], preferred_element_type=F32))
    mix = (jnp.concatenate(ys, axis=-1) + bpool_ref[...]) * pscale_ref[...]

    h1 = _layer_norm(DEEPNORM_ALPHA * cur + mix, lng_ref[0:1, :], lnb_ref[0:1, :])
    m = _mlp(h1.astype(BF16), wup_ref, bup_ref, wdown_ref) + bdown_ref[...]
    h2 = _layer_norm(DEEPNORM_ALPHA * h1 + m, lng_ref[1:2, :], lnb_ref[1:2, :])
    h_out[0] = h2

    h2b = h2.astype(BF16)
    kt = lax.dot_general(wkt_ref[...], h2b, _NT_DIMS, preferred_element_type=F32)
    kt_out[0] = kt
    ktb_out[0, 0] = kt.astype(BF16)
    vt = lax.dot_general(wvt_ref[...], h2b, _NT_DIMS, preferred_element_type=F32)
    vt_out[0] = vt
    vtb_out[0, 0] = vt.astype(BF16)
    q = jnp.dot(h2b, wq_ref[...], preferred_element_type=F32)
    qb_out[0] = (q * (HEAD_DIM ** -0.5)).astype(BF16)


def _const_spec(shape):
    nd = len(shape)
    return pl.BlockSpec(shape, lambda *_: (0,) * nd, pipeline_mode=pl.Buffered(1))


def _layer0_call(u, meta, n_rows_out, wts, *, prompt, tm=ROW_TILE, stride=1):
    nb = u.shape[0]
    halo = HALO * stride
    if prompt:
        seq = u.shape[1]
        assert seq % tm == 0 and N_META == HALO and stride == 1
        rows = seq + tm
        window = lambda size, lag: pl.BlockSpec(
            (pl.Element(1), pl.Element(size), pl.Element(D_MODEL)),
            lambda b, i: (b, pl.multiple_of(jnp.clip(i * tm - lag, 0, seq - size), F32_SUBLANES), 0))
        in_row_spec = window(tm, N_META)
        prev_spec = window(halo, N_META + halo)
    else:
        rows = tm
        assert halo % tm == 0
        in_row_spec = pl.BlockSpec((1, tm, D_MODEL), lambda b, i: (b, halo // tm, 0))
        prev_spec = pl.BlockSpec((1, halo, D_MODEL), lambda b, i: (b, 0, 0))
    nt = rows // tm
    row_spec = pl.BlockSpec((1, tm, D_MODEL), lambda b, i: (b, i, 0))
    col_spec = pl.BlockSpec((1, D_MODEL, tm), lambda b, i: (b, 0, i))
    tile_spec = pl.BlockSpec((1, 1, D_MODEL, tm), lambda b, i: (b, i, 0, 0))
    in_specs = [prev_spec, in_row_spec, _const_spec((N_META, D_MODEL)),
                _const_spec((len(POOL_WINDOWS), POOL_GROUP, POOL_GROUP)),
                _const_spec((1, D_MODEL)), _const_spec((1, D_MODEL)),
                _const_spec((2, D_MODEL)), _const_spec((2, D_MODEL)),
                _const_spec((D_MODEL, D_FF)), _const_spec((1, D_FF)),
                _const_spec((D_FF, D_MODEL)), _const_spec((1, D_MODEL)),
                _const_spec((D_MODEL, D_MODEL)), _const_spec((D_MODEL, D_MODEL)),
                _const_spec((D_MODEL, D_MODEL))]
    out_shape = (jax.ShapeDtypeStruct((nb, rows, D_MODEL), F32),
                 jax.ShapeDtypeStruct((nb, D_MODEL, n_rows_out), F32),
                 jax.ShapeDtypeStruct((nb, D_MODEL, n_rows_out), F32),
                 jax.ShapeDtypeStruct((nb, rows, D_MODEL), BF16),
                 jax.ShapeDtypeStruct((nb, nt, D_MODEL, tm), BF16),
                 jax.ShapeDtypeStruct((nb, nt, D_MODEL, tm), BF16))
    return pl.pallas_call(
        functools.partial(_layer0_kernel, prompt=prompt, tm=tm, stride=stride),
        grid=(nb, nt),
        in_specs=in_specs,
        out_specs=(row_spec, col_spec, col_spec, row_spec, tile_spec, tile_spec),
        out_shape=out_shape,
        scratch_shapes=[pltpu.VMEM((halo + tm, D_MODEL), F32)],
        compiler_params=pltpu.CompilerParams(dimension_semantics=("parallel", "arbitrary"),
                                             vmem_limit_bytes=VMEM_LIMIT_BYTES),
        name="layer0",
    )(u, u, meta, *wts)


def _layer1_kernel(o_ref, h_ref, wo_ref, lng_ref, lnb_ref, wup_ref, bup_ref, wdown_ref, bdown_ref, y_out):
    h = h_ref[...]
    mix = jnp.dot(o_ref[...], wo_ref[...], preferred_element_type=F32)
    h1 = _layer_norm(DEEPNORM_ALPHA * h + mix, lng_ref[0:1, :], lnb_ref[0:1, :])
    m = _mlp(h1.astype(BF16), wup_ref, bup_ref, wdown_ref) + bdown_ref[...]
    y_out[...] = _layer_norm(DEEPNORM_ALPHA * h1 + m, lng_ref[1:2, :], lnb_ref[1:2, :])


def _layer1_call(o, h, wts, tm, n_tiles=None, in_row_start=None):
    if in_row_start is None:
        n_tiles = h.shape[0] // tm
        in_spec = pl.BlockSpec((tm, D_MODEL), lambda i: (i, 0))
    else:
        in_spec = pl.BlockSpec((pl.Element(tm), pl.Element(D_MODEL)),
                               lambda i: (pl.multiple_of(in_row_start(i), BF16_SUBLANES), 0))
    rows = n_tiles * tm
    row_spec = pl.BlockSpec((tm, D_MODEL), lambda i: (i, 0))
    in_specs = [in_spec, in_spec,
                _const_spec((D_MODEL, D_MODEL)),
                _const_spec((2, D_MODEL)), _const_spec((2, D_MODEL)),
                _const_spec((D_MODEL, D_FF)), _const_spec((1, D_FF)),
                _const_spec((D_FF, D_MODEL)), _const_spec((1, D_MODEL))]
    return pl.pallas_call(
        _layer1_kernel,
        grid=(rows // tm,),
        in_specs=in_specs,
        out_specs=row_spec,
        out_shape=jax.ShapeDtypeStruct((rows, D_MODEL), F32),
        compiler_params=pltpu.CompilerParams(dimension_semantics=("parallel",),
                                             vmem_limit_bytes=VMEM_LIMIT_BYTES),
        name="layer1",
    )(o, h, *wts)


SOFTPLUS_LINEAR_ABOVE = 80.0


def _softplus(z):
    return jnp.maximum(jnp.log(1.0 + jnp.exp(jnp.minimum(z, SOFTPLUS_LINEAR_ABOVE))), z)


def _suffix_sum(sp, tri):
    return jnp.dot(sp.astype(BF16), tri, preferred_element_type=F32)


def _attn_prompt_kernel(bias_ref, q_ref, k_ref, v_ref, tri_ref, *refs, tq, first_tile):
    o_ref, qh_buf, z_buf, a_buf, acc_ref, carry_ref = refs[-6:]
    g = pl.program_id(1)
    i = pl.program_id(2) + first_tile
    tk, nh = ATT_TK, ATT_HEADS
    lane = lax.broadcasted_iota(jnp.int32, (tq, LANES), 1)
    pair_sub = [divmod(h, 2) for h in range(nh)]
    for h, (pr, sub) in enumerate(pair_sub):
        qp = q_ref[0, :, pr * LANES:(pr + 1) * LANES]
        qh_buf[h] = jnp.where(lane // HEAD_DIM == sub, qp, jnp.zeros_like(qp))
    acc_ref[...] = jnp.zeros_like(acc_ref)
    carry_ref[...] = jnp.zeros_like(carry_ref)

    def key_tile(p):
        return jnp.maximum(i - p, 0)

    def qk(h, t):
        pr = pair_sub[h][0]
        kt = k_ref[0, t, pr * LANES:(pr + 1) * LANES, :]
        return jnp.dot(qh_buf[h], kt, preferred_element_type=F32)

    def av(h, t):
        pr = pair_sub[h][0]
        vt = v_ref[0, t, pr * LANES:(pr + 1) * LANES, :]
        return lax.dot_general(a_buf[h], vt, _NT_DIMS, preferred_element_type=F32)

    def weights(h, z, s, mask):
        c = carry_ref[h]
        suffix = s + jnp.concatenate([c] * (tk // LANES), axis=1)
        a = jnp.exp(z - suffix)
        if mask is not None:
            a = jnp.where(mask, a, 0.0)
        a_buf[h] = a.astype(BF16)
        carry_ref[h] = jnp.broadcast_to(suffix[:, 0:1], (tq, LANES))

    def sweep(p1, p3, do1, do2, do3, masked):
        tri = tri_ref[...]
        mask = None
        if masked:
            row = lax.broadcasted_iota(jnp.int32, (tq, tk), 0)
            col = lax.broadcasted_iota(jnp.int32, (tq, tk), 1)
            mask = col < row
        t1, t3 = key_tile(p1), key_tile(p3)
        zs, ss, avs, qks = {}, {}, {}, {}
        for k in range(nh + 2):
            if k < nh:
                if do3:
                    avs[k] = av(k, t3)
                if do2:
                    z = z_buf[k]
                    sp = _softplus(z)
                    if masked:
                        sp = jnp.where(mask, sp, 0.0)
                    zs[k], ss[k] = z, _suffix_sum(sp, tri)
            h = k - 1
            if 0 <= h < nh:
                if do3:
                    acc_ref[h] += avs.pop(h)
                if do2:
                    weights(h, zs.pop(h), ss.pop(h), mask)
                if do1:
                    qks[h] = qk(h, t1)
            h = k - 2
            if 0 <= h < nh and do1:
                z_buf[h] = qks.pop(h) + bias_ref[nh * g + h]

    sweep(0, 0, True, False, False, False)
    sweep(1, 0, True, True, False, True)

    def body(n, _):
        sweep(n, n - 2, True, True, True, False)
        return 0

    lax.fori_loop(2, i + 1, body, 0)

    @pl.when(i >= 1)
    def _():
        sweep(0, i - 1, False, True, True, False)

    sweep(0, i, False, False, True, False)
    if tq < ATT_TQ:
        o_ref[0, tq:, :] = jnp.zeros((ATT_TQ - tq, nh * HEAD_DIM), o_ref.dtype)
    for pr in range(nh // 2):
        o_ref[0, 0:tq, pr * LANES:(pr + 1) * LANES] = jnp.where(
            lane // HEAD_DIM == 0, acc_ref[2 * pr], acc_ref[2 * pr + 1]).astype(o_ref.dtype)


def _attn_prompt_call(bias, qb, ktb, vtb, tri, t_real):
    nb, rows, _ = qb.shape
    nk = ktb.shape[1]
    width = ATT_HEADS * HEAD_DIM
    n_full, tail = divmod(t_real, ATT_TQ)
    tq_tail = -(-tail // BF16_SUBLANES) * BF16_SUBLANES
    if tail == 0 or ATT_TQ % tq_tail != 0:
        n_full, tail = rows // ATT_TQ, 0
    kv_spec = pl.BlockSpec((1, nk, width, ATT_TK), lambda b, g, i: (b, 0, g, 0), pipeline_mode=pl.Buffered(1))
    tri_spec = pl.BlockSpec((ATT_TK, ATT_TK), lambda b, g, i: (0, 0))
    o_spec = lambda first: pl.BlockSpec((1, ATT_TQ, width), lambda b, g, i: (b, i + first, g))

    def call(tq, first_tile, n_tiles, q_spec, extra_in, extra_specs, aliases):
        return pl.pallas_call(
            functools.partial(_attn_prompt_kernel, tq=tq, first_tile=first_tile),
            grid=(nb, N_HEADS // ATT_HEADS, n_tiles),
            in_specs=[pl.BlockSpec(memory_space=pltpu.SMEM), q_spec, kv_spec, kv_spec, tri_spec] + extra_specs,
            out_specs=o_spec(first_tile),
            out_shape=jax.ShapeDtypeStruct((nb, rows, D_MODEL), BF16),
            scratch_shapes=[pltpu.VMEM((ATT_HEADS, tq, LANES), BF16),
                            pltpu.VMEM((ATT_HEADS, tq, ATT_TK), F32),
                            pltpu.VMEM((ATT_HEADS, tq, ATT_TK), BF16),
                            pltpu.VMEM((ATT_HEADS, tq, LANES), F32),
                            pltpu.VMEM((ATT_HEADS, tq, LANES), F32)],
            input_output_aliases=aliases,
            compiler_params=pltpu.CompilerParams(dimension_semantics=("parallel", "parallel", "arbitrary"),
                                                 vmem_limit_bytes=VMEM_LIMIT_BYTES),
            name="attn_prompt" if tq == ATT_TQ else "attn_prompt_tail",
        )(bias, qb, ktb, vtb, tri, *extra_in)

    o = call(ATT_TQ, 0, n_full, pl.BlockSpec((1, ATT_TQ, width), lambda b, g, i: (b, i, g)), [], [], {})
    if tail:
        per_tile = ATT_TQ // tq_tail
        q_tail = pl.BlockSpec((1, tq_tail, width), lambda b, g, i: (b, n_full * per_tile, g))
        o = call(tq_tail, n_full, 1, q_tail, [o], [pl.BlockSpec(memory_space=pl.ANY)], {5: 0})
    return o


def _attn_sample_kernel(pt_ref, qbd_ref, bias_ref, knew_ref, vnew_ref, *refs, n_new, npp):
    del pt_ref
    kpages, vpages = refs[:npp], refs[npp:2 * npp]
    tri_ref, selt_ref, fold_ref, o_ref, kcat, vcat, a_buf, acc_t, carry_ref = refs[2 * npp:]
    j = pl.program_id(1)
    nrow = N_HEADS * n_new
    chunk = ATT_TK
    n_chunks = npp * PAGE_SIZE // chunk
    qbd = qbd_ref[0]
    tri = tri_ref[...]

    @pl.when(j == 0)
    def _():
        qi = lax.broadcasted_iota(jnp.int32, (nrow, PAGE_SIZE), 0) // N_HEADS
        slot = lax.broadcasted_iota(jnp.int32, (nrow, PAGE_SIZE), 1)
        mask = slot < qi
        z = jnp.dot(qbd, knew_ref[0], preferred_element_type=F32) + bias_ref[:, 0:PAGE_SIZE]
        sp = jnp.where(mask, _softplus(z), 0.0)
        suffix = _suffix_sum(sp, tri[0:PAGE_SIZE, 0:PAGE_SIZE])
        a = jnp.where(mask, jnp.exp(z - suffix), 0.0).astype(BF16)
        acc_t[...] = lax.dot_general(vnew_ref[0], a, _NT_DIMS, preferred_element_type=F32)
        carry_ref[...] = jnp.broadcast_to(suffix[:, 0:1], carry_ref.shape)
        a_buf[...] = jnp.zeros_like(a_buf)

    for k in range(npp):
        vcat[:, k * PAGE_SIZE:(k + 1) * PAGE_SIZE] = vpages[k][0].astype(BF16)
    acc_t[...] += lax.dot_general(vcat[...], a_buf[...], _NT_DIMS, preferred_element_type=F32)

    for k in range(npp):
        kcat[:, k * PAGE_SIZE:(k + 1) * PAGE_SIZE] = kpages[k][0].astype(BF16)
    z = jnp.dot(qbd, kcat[...], preferred_element_type=F32) + bias_ref[...]
    sp = _softplus(z)
    carry = carry_ref[...]
    for c in reversed(range(n_chunks)):
        lanes = slice(c * chunk, (c + 1) * chunk)
        suffix = _suffix_sum(sp[:, lanes], tri) + jnp.concatenate([carry] * (chunk // LANES), axis=1)
        a_buf[:, lanes] = jnp.exp(z[:, lanes] - suffix).astype(BF16)
        carry = jnp.broadcast_to(suffix[:, 0:1], carry.shape)
    carry_ref[...] = carry

    @pl.when(j == pl.num_programs(1) - 1)
    def _():
        picked = (acc_t[...] * selt_ref[...]).astype(BF16)
        o = lax.dot_general(fold_ref[...], picked, _NT_DIMS, preferred_element_type=F32)
        o_ref[0] = o[0:n_new].astype(o_ref.dtype)


def _attn_sample_call(page_table, qbd, bias_rows, knew, vnew, cache_kt, cache_vt, tri, selt, fold, n_new):
    nseq, n_pages = page_table.shape
    npp = SAMPLE_PAGES
    nsteps = n_pages // npp
    nrow = N_HEADS * n_new

    def page_map(k, shift):
        def index_map(b, j, pt):
            step = jnp.clip(j - shift, 0, nsteps - 1)
            return (pt[b, n_pages - (step + 1) * npp + k], 0, 0)
        return index_map

    seq3 = lambda b, j, pt: (b, 0, 0)
    const2 = lambda b, j, pt: (0, 0)
    page_block = (1, D_MODEL, PAGE_SIZE)
    grid_spec = pltpu.PrefetchScalarGridSpec(
        num_scalar_prefetch=1,
        grid=(nseq, nsteps + 1),
        in_specs=([pl.BlockSpec((1, nrow, D_MODEL), seq3),
                   pl.BlockSpec((nrow, npp * PAGE_SIZE), const2),
                   pl.BlockSpec(page_block, seq3),
                   pl.BlockSpec(page_block, seq3)]
                  + [pl.BlockSpec(page_block, page_map(k, 0)) for k in range(npp)]
                  + [pl.BlockSpec(page_block, page_map(k, 1)) for k in range(npp)]
                  + [pl.BlockSpec((ATT_TK, ATT_TK), const2),
                     pl.BlockSpec((D_MODEL, nrow), const2),
                     pl.BlockSpec((HALO, nrow), const2)]),
        out_specs=pl.BlockSpec((1, n_new, D_MODEL), seq3),
        scratch_shapes=[pltpu.VMEM((D_MODEL, npp * PAGE_SIZE), BF16),
                        pltpu.VMEM((D_MODEL, npp * PAGE_SIZE), BF16),
                        pltpu.VMEM((nrow, npp * PAGE_SIZE), BF16),
                        pltpu.VMEM((D_MODEL, nrow), F32),
                        pltpu.VMEM((nrow, LANES), F32)])
    return pl.pallas_call(
        functools.partial(_attn_sample_kernel, n_new=n_new, npp=npp),
        grid_spec=grid_spec,
        out_shape=jax.ShapeDtypeStruct((nseq, n_new, D_MODEL), BF16),
        compiler_params=pltpu.CompilerParams(dimension_semantics=("parallel", "arbitrary"),
                                             vmem_limit_bytes=VMEM_LIMIT_BYTES),
        name="attn_sample",
    )(page_table, qbd, bias_rows, knew, vnew, *([cache_kt] * npp), *([cache_vt] * npp), tri, selt, fold)


def _suffix_matrix(n):
    j = jnp.arange(n)[:, None]
    s = jnp.arange(n)[None, :]
    return (j >= s).astype(BF16)


def kernel(x_prompt, x_sample, cache_k, cache_v, state_pool, page_table, meta_tokens, ln_g, ln_b, w_pool, b_pool,
           pool_scale, w_up, b_up, w_down, b_down, w_kv, w_q, w_o, sb_bias):
    nb, seq, _ = x_prompt.shape
    nseq, n_new, _ = x_sample.shape
    t_real = seq + N_META
    t_pad = -(-t_real // ATT_TK) * ATT_TK
    assert n_new <= HALO and page_table.shape[1] % SAMPLE_PAGES == 0

    wts0 = (w_pool[0].astype(BF16), b_pool[0:1], pool_scale[0:1], ln_g[0], ln_b[0],
            w_up[0].astype(BF16), b_up[0:1], w_down[0].astype(BF16), b_down[0:1],
            w_kv[:, :D_MODEL].T.astype(BF16), w_kv[:, D_MODEL:].T.astype(BF16), w_q[0].astype(BF16))
    wts1 = (w_o[0].astype(BF16), ln_g[1], ln_b[1], w_up[1].astype(BF16), b_up[1:2],
            w_down[1].astype(BF16), b_down[1:2])
    bias = sb_bias[0].astype(F32)
    tri = _suffix_matrix(ATT_TK)

    h_p, kt_p, vt_p, qb_p, ktb_p, vtb_p = _layer0_call(x_prompt, meta_tokens, t_real, wts0, prompt=True)
    o_p = _attn_prompt_call(bias, qb_p, ktb_p, vtb_p, tri, t_real)
    per_seq = seq // ROW_TILE
    start = lambda i: (i // per_seq) * t_pad + N_META + (i % per_seq) * ROW_TILE
    y_prompt = _layer1_call(o_p.reshape(nb * t_pad, D_MODEL), h_p.reshape(nb * t_pad, D_MODEL), wts1, ROW_TILE,
                            nb * per_seq, start).reshape(nb, seq, D_MODEL)
    pool_prompt = x_prompt[None, :, seq - POOL_CTX:, :]

    ctx = state_pool[0]
    u_s = jnp.concatenate([jnp.zeros((HALO - POOL_CTX, nseq, D_MODEL), F32), ctx.transpose(1, 0, 2),
                           x_sample.transpose(1, 0, 2)], axis=0)
    n_s = n_new * nseq
    h_s, kt_s, vt_s, qb_s, _, _ = _layer0_call(u_s.reshape(1, (HALO + n_new) * nseq, D_MODEL), meta_tokens, n_s,
                                               wts0, prompt=False, tm=n_s, stride=nseq)
    h_s = h_s.reshape(n_s, D_MODEL)
    qb_s = qb_s.reshape(n_new, nseq, D_MODEL).transpose(1, 0, 2)
    take_t = lambda a: a.reshape(D_MODEL, n_new, nseq).transpose(2, 0, 1)
    kt_s, vt_s = take_t(kt_s), take_t(vt_s)

    nrow = N_HEADS * n_new
    head_of_lane = jnp.arange(D_MODEL) // HEAD_DIM
    sel = head_of_lane[None, :] == (jnp.arange(nrow) % N_HEADS)[:, None]
    qbd = jnp.where(sel[None], jnp.repeat(qb_s, N_HEADS, axis=1), jnp.zeros((), BF16))
    bias_rows = jnp.broadcast_to(jnp.tile(bias, n_new)[:, None], (nrow, SAMPLE_PAGES * PAGE_SIZE))
    fold = (jnp.arange(HALO)[:, None] == (jnp.arange(nrow) // N_HEADS)[None, :]).astype(BF16)
    pad_new = lambda a: jnp.pad(a.astype(BF16), ((0, 0), (0, 0), (0, PAGE_SIZE - n_new)))
    n_phys = cache_k.shape[0]
    pages_t = lambda c: c.transpose(0, 2, 3, 1).reshape(n_phys, D_MODEL, PAGE_SIZE)
    o_s = _attn_sample_call(page_table, qbd, bias_rows, pad_new(kt_s), pad_new(vt_s),
                            pages_t(cache_k), pages_t(cache_v), tri, sel.T.astype(F32), fold, n_new)
    y_s = _layer1_call(o_s.transpose(1, 0, 2).reshape(n_s, D_MODEL), h_s, wts1, n_s)
    y_sample = y_s.reshape(n_new, nseq, D_MODEL).transpose(1, 0, 2)
    pool_sample = jnp.concatenate([ctx, x_sample], axis=1)[None, :, n_new:, :]

    rows = lambda a: a.reshape(a.shape[0], N_HEADS, HEAD_DIM, a.shape[2]).transpose(0, 3, 1, 2)
    return (y_prompt, y_sample, rows(kt_p), rows(vt_p), rows(kt_s), rows(vt_s), pool_prompt, pool_sample)
```

```python
import functools

import jax
import jax.numpy as jnp
from jax import lax
from jax.experimental import pallas as pl
from jax.experimental.pallas import tpu as pltpu

D_MODEL = 1024
N_META = 16
POOL_WINDOWS = (2, 4, 8, 16)
POOL_GROUP = D_MODEL // len(POOL_WINDOWS)
POOL_CTX = max(POOL_WINDOWS) - 1
HALO = 16
HEAD_DIM = 64
N_HEADS = D_MODEL // HEAD_DIM
D_FF = 4 * D_MODEL
LN_EPS = 1e-5
DEPTH = 2
DEEPNORM_ALPHA = (2.0 * DEPTH) ** 0.25
PAGE_SIZE = 128

ROW_TILE = 256
ATT_TQ = 256
ATT_TK = 256
ATT_HEADS = 16
SAMPLE_PAGES = 16
LANES = 128
BF16_SUBLANES = 16
F32_SUBLANES = 8
VMEM_LIMIT_BYTES = 56 * 1024 * 1024

F32 = jnp.float32
BF16 = jnp.bfloat16
_NT_DIMS = (((1,), (1,)), ((), ()))


def _layer_norm(x, g, b):
    mu = jnp.mean(x, axis=-1, keepdims=True)
    xc = x - mu
    var = jnp.mean(xc * xc, axis=-1, keepdims=True)
    return xc * lax.rsqrt(var + LN_EPS) * g + b


def _mlp(hb, wup_ref, bup_ref, wdown_ref):
    acc = None
    for c in range(D_FF // D_MODEL):
        lo, hi = c * D_MODEL, (c + 1) * D_MODEL
        a = jnp.dot(hb, wup_ref[:, lo:hi], preferred_element_type=F32) + bup_ref[:, lo:hi]
        a = jnp.maximum(a, 0.0)
        a2 = (a * a).astype(BF16)
        part = jnp.dot(a2, wdown_ref[lo:hi, :], preferred_element_type=F32)
        acc = part if acc is None else acc + part
    return acc


def _layer0_kernel(prev_ref, cur_ref, meta_ref, wpool_ref, bpool_ref, pscale_ref, lng_ref, lnb_ref,
                   wup_ref, bup_ref, wdown_ref, bdown_ref, wkt_ref, wvt_ref, wq_ref,
                   h_out, kt_out, vt_out, qb_out, ktb_out, vtb_out, ext_ref, *, prompt, tm, stride):
    i = pl.program_id(1)
    halo = HALO * stride
    cur = cur_ref[0]
    if prompt:
        body = tm - N_META
        first_tile = jnp.concatenate([meta_ref[...], cur[0:body]], axis=0)
        last_tile = jnp.concatenate([cur[body:tm], jnp.zeros((body, D_MODEL), F32)], axis=0)
        cur = jnp.where(i == 0, first_tile, jnp.where(i == pl.num_programs(1) - 1, last_tile, cur))
        prev = jnp.where(i == 0, 0.0, prev_ref[0])
        pos = i * tm + lax.broadcasted_iota(jnp.int32, (tm, 1), 0)
    else:
        prev = prev_ref[0]
    ext_ref[0:halo, :] = prev
    ext_ref[halo:halo + tm, :] = cur

    ys = []
    for g, w in enumerate(POOL_WINDOWS):
        c0, c1 = g * POOL_GROUP, (g + 1) * POOL_GROUP
        s = cur[:, c0:c1]
        for k in range(1, w):
            s = s + ext_ref[halo - k * stride:halo - k * stride + tm, c0:c1]
        if prompt:
            inv_cnt = 1.0 / jnp.minimum(pos + 1, w).astype(F32)
        else:
            inv_cnt = 1.0 / w
        p = s * inv_cnt - cur[:, c0:c1]
        ys.append(jnp.dot(p.astype(BF16), wpool_ref[g], preferred_element_type=F32))
    mix = (jnp.concatenate(ys, axis=-1) + bpool_ref[...]) * pscale_ref[...]

    h1 = _layer_norm(DEEPNORM_ALPHA * cur + mix, lng_ref[0:1, :], lnb_ref[0:1, :])
    m = _mlp(h1.astype(BF16), wup_ref, bup_ref, wdown_ref) + bdown_ref[...]
    h2 = _layer_norm(DEEPNORM_ALPHA * h1 + m, lng_ref[1:2, :], lnb_ref[1:2, :])
    h_out[0] = h2

    h2b = h2.astype(BF16)
    kt = lax.dot_general(wkt_ref[...], h2b, _NT_DIMS, preferred_element_type=F32)
    kt_out[0] = kt
    ktb_out[0, 0] = kt.astype(BF16)
    vt = lax.dot_general(wvt_ref[...], h2b, _NT_DIMS, preferred_element_type=F32)
    vt_out[0] = vt
    vtb_out[0, 0] = vt.astype(BF16)
    q = jnp.dot(h2b, wq_ref[...], preferred_element_type=F32)
    qb_out[0] = (q * (HEAD_DIM ** -0.5)).astype(BF16)


def _const_spec(shape):
    nd = len(shape)
    return pl.BlockSpec(shape, lambda *_: (0,) * nd, pipeline_mode=pl.Buffered(1))


def _layer0_call(u, meta, n_rows_out, wts, *, prompt, tm=ROW_TILE, stride=1):
    nb = u.shape[0]
    halo = HALO * stride
    if prompt:
        seq = u.shape[1]
        assert seq % tm == 0 and N_META == HALO and stride == 1
        rows = seq + tm
        window = lambda size, lag: pl.BlockSpec(
            (pl.Element(1), pl.Element(size), pl.Element(D_MODEL)),
            lambda b, i: (b, pl.multiple_of(jnp.clip(i * tm - lag, 0, seq - size), F32_SUBLANES), 0))
        in_row_spec = window(tm, N_META)
        prev_spec = window(halo, N_META + halo)
    else:
        rows = tm
        assert halo % tm == 0
        in_row_spec = pl.BlockSpec((1, tm, D_MODEL), lambda b, i: (b, halo // tm, 0))
        prev_spec = pl.BlockSpec((1, halo, D_MODEL), lambda b, i: (b, 0, 0))
    nt = rows // tm
    row_spec = pl.BlockSpec((1, tm, D_MODEL), lambda b, i: (b, i, 0))
    col_spec = pl.BlockSpec((1, D_MODEL, tm), lambda b, i: (b, 0, i))
    tile_spec = pl.BlockSpec((1, 1, D_MODEL, tm), lambda b, i: (b, i, 0, 0))
    in_specs = [prev_spec, in_row_spec, _const_spec((N_META, D_MODEL)),
                _const_spec((len(POOL_WINDOWS), POOL_GROUP, POOL_GROUP)),
                _const_spec((1, D_MODEL)), _const_spec((1, D_MODEL)),
                _const_spec((2, D_MODEL)), _const_spec((2, D_MODEL)),
                _const_spec((D_MODEL, D_FF)), _const_spec((1, D_FF)),
                _const_spec((D_FF, D_MODEL)), _const_spec((1, D_MODEL)),
                _const_spec((D_MODEL, D_MODEL)), _const_spec((D_MODEL, D_MODEL)),
                _const_spec((D_MODEL, D_MODEL))]
    out_shape = (jax.ShapeDtypeStruct((nb, rows, D_MODEL), F32),
                 jax.ShapeDtypeStruct((nb, D_MODEL, n_rows_out), F32),
                 jax.ShapeDtypeStruct((nb, D_MODEL, n_rows_out), F32),
                 jax.ShapeDtypeStruct((nb, rows, D_MODEL), BF16),
                 jax.ShapeDtypeStruct((nb, nt, D_MODEL, tm), BF16),
                 jax.ShapeDtypeStruct((nb, nt, D_MODEL, tm), BF16))
    return pl.pallas_call(
        functools.partial(_layer0_kernel, prompt=prompt, tm=tm, stride=stride),
        grid=(nb, nt),
        in_specs=in_specs,
        out_specs=(row_spec, col_spec, col_spec, row_spec, tile_spec, tile_spec),
        out_shape=out_shape,
        scratch_shapes=[pltpu.VMEM((halo + tm, D_MODEL), F32)],
        compiler_params=pltpu.CompilerParams(dimension_semantics=("parallel", "arbitrary"),
                                             vmem_limit_bytes=VMEM_LIMIT_BYTES),
        name="layer0",
    )(u, u, meta, *wts)


def _layer1_kernel(o_ref, h_ref, wo_ref, lng_ref, lnb_ref, wup_ref, bup_ref, wdown_ref, bdown_ref, y_out):
    h = h_ref[...]
    mix = jnp.dot(o_ref[...], wo_ref[...], preferred_element_type=F32)
    h1 = _layer_norm(DEEPNORM_ALPHA * h + mix, lng_ref[0:1, :], lnb_ref[0:1, :])
    m = _mlp(h1.astype(BF16), wup_ref, bup_ref, wdown_ref) + bdown_ref[...]
    y_out[...] = _layer_norm(DEEPNORM_ALPHA * h1 + m, lng_ref[1:2, :], lnb_ref[1:2, :])


def _layer1_call(o, h, wts, tm, n_tiles=None, in_row_start=None):
    if in_row_start is None:
        n_tiles = h.shape[0] // tm
        in_spec = pl.BlockSpec((tm, D_MODEL), lambda i: (i, 0))
    else:
        in_spec = pl.BlockSpec((pl.Element(tm), pl.Element(D_MODEL)),
                               lambda i: (pl.multiple_of(in_row_start(i), BF16_SUBLANES), 0))
    rows = n_tiles * tm
    row_spec = pl.BlockSpec((tm, D_MODEL), lambda i: (i, 0))
    in_specs = [in_spec, in_spec,
                _const_spec((D_MODEL, D_MODEL)),
                _const_spec((2, D_MODEL)), _const_spec((2, D_MODEL)),
                _const_spec((D_MODEL, D_FF)), _const_spec((1, D_FF)),
                _const_spec((D_FF, D_MODEL)), _const_spec((1, D_MODEL))]
    return pl.pallas_call(
        _layer1_kernel,
        grid=(rows // tm,),
        in_specs=in_specs,
        out_specs=row_spec,
        out_shape=jax.ShapeDtypeStruct((rows, D_MODEL), F32),
        compiler_params=pltpu.CompilerParams(dimension_semantics=("parallel",),
                                             vmem_limit_bytes=VMEM_LIMIT_BYTES),
        name="layer1",
    )(o, h, *wts)


SOFTPLUS_LINEAR_ABOVE = 80.0


def _softplus(z):
    return jnp.maximum(jnp.log(1.0 + jnp.exp(jnp.minimum(z, SOFTPLUS_LINEAR_ABOVE))), z)


def _suffix_sum(sp, tri):
    return jnp.dot(sp.astype(BF16), tri, preferred_element_type=F32)


def _attn_prompt_kernel(bias_ref, q_ref, k_ref, v_ref, tri_ref, o_ref, qh_buf, z_buf, a_buf, acc_ref, carry_ref,
                        *, n_full, tail):
    i = pl.program_id(2)

    @pl.when(i < n_full)
    def _():
        _attn_prompt_tile(ATT_TQ, bias_ref, q_ref, k_ref, v_ref, tri_ref, o_ref,
                          qh_buf, z_buf, a_buf, acc_ref, carry_ref)

    if tail:
        @pl.when(i == n_full)
        def _():
            _attn_prompt_tile(tail, bias_ref, q_ref, k_ref, v_ref, tri_ref, o_ref,
                              qh_buf, z_buf, a_buf, acc_ref, carry_ref)


def _attn_prompt_tile(tq, bias_ref, q_ref, k_ref, v_ref, tri_ref, o_ref, qh_buf, z_buf, a_buf, acc_ref, carry_ref):
    g = pl.program_id(1)
    i = pl.program_id(2)
    tk, nh = ATT_TK, ATT_HEADS
    rows = slice(0, tq)
    lane = lax.broadcasted_iota(jnp.int32, (tq, LANES), 1)
    pair_sub = [divmod(h, 2) for h in range(nh)]
    for h, (pr, sub) in enumerate(pair_sub):
        qp = q_ref[0, rows, pr * LANES:(pr + 1) * LANES]
        qh_buf[h, rows] = jnp.where(lane // HEAD_DIM == sub, qp, jnp.zeros_like(qp))
    acc_ref[:, rows] = jnp.zeros((nh, tq, LANES), F32)
    carry_ref[:, rows] = jnp.zeros((nh, tq, LANES), F32)

    def key_tile(p):
        return jnp.maximum(i - p, 0)

    def qk(h, t):
        pr = pair_sub[h][0]
        kt = k_ref[0, t, pr * LANES:(pr + 1) * LANES, :]
        return jnp.dot(qh_buf[h, rows], kt, preferred_element_type=F32)

    def av(h, t):
        pr = pair_sub[h][0]
        vt = v_ref[0, t, pr * LANES:(pr + 1) * LANES, :]
        return lax.dot_general(a_buf[h, rows], vt, _NT_DIMS, preferred_element_type=F32)

    def weights(h, z, s, mask):
        c = carry_ref[h, rows]
        suffix = s + jnp.concatenate([c] * (tk // LANES), axis=1)
        a = jnp.exp(z - suffix)
        if mask is not None:
            a = jnp.where(mask, a, 0.0)
        a_buf[h, rows] = a.astype(BF16)
        carry_ref[h, rows] = jnp.broadcast_to(suffix[:, 0:1], (tq, LANES))

    def sweep(p1, p3, do1, do2, do3, masked):
        tri = tri_ref[...]
        mask = None
        if masked:
            row = lax.broadcasted_iota(jnp.int32, (tq, tk), 0)
            col = lax.broadcasted_iota(jnp.int32, (tq, tk), 1)
            mask = col < row
        t1, t3 = key_tile(p1), key_tile(p3)
        zs, ss, avs, qks = {}, {}, {}, {}
        for k in range(nh + 2):
            if k < nh:
                if do3:
                    avs[k] = av(k, t3)
                if do2:
                    z = z_buf[k, rows]
                    sp = _softplus(z)
                    if masked:
                        sp = jnp.where(mask, sp, 0.0)
                    zs[k], ss[k] = z, _suffix_sum(sp, tri)
            h = k - 1
            if 0 <= h < nh:
                if do3:
                    acc_ref[h, rows] += avs.pop(h)
                if do2:
                    weights(h, zs.pop(h), ss.pop(h), mask)
                if do1:
                    qks[h] = qk(h, t1)
            h = k - 2
            if 0 <= h < nh and do1:
                z_buf[h, rows] = qks.pop(h) + bias_ref[nh * g + h]

    sweep(0, 0, True, False, False, False)
    sweep(1, 0, True, True, False, True)

    def body(n, _):
        sweep(n, n - 2, True, True, True, False)
        return 0

    lax.fori_loop(2, i + 1, body, 0)

    @pl.when(i >= 1)
    def _():
        sweep(0, i - 1, False, True, True, False)

    sweep(0, i, False, False, True, False)
    if tq < ATT_TQ:
        o_ref[0, tq:, :] = jnp.zeros((ATT_TQ - tq, nh * HEAD_DIM), o_ref.dtype)
    for pr in range(nh // 2):
        o_ref[0, rows, pr * LANES:(pr + 1) * LANES] = jnp.where(
            lane // HEAD_DIM == 0, acc_ref[2 * pr, rows], acc_ref[2 * pr + 1, rows]).astype(o_ref.dtype)


def _attn_prompt_call(bias, qb, ktb, vtb, tri, t_real):
    nb, rows, _ = qb.shape
    nk = ktb.shape[1]
    width = ATT_HEADS * HEAD_DIM
    n_full, tail = divmod(t_real, ATT_TQ)
    tail = -(-tail // BF16_SUBLANES) * BF16_SUBLANES
    q_spec = pl.BlockSpec((1, ATT_TQ, width), lambda b, g, i: (b, i, g))
    kv_spec = pl.BlockSpec((1, nk, width, ATT_TK), lambda b, g, i: (b, 0, g, 0), pipeline_mode=pl.Buffered(1))
    return pl.pallas_call(
        functools.partial(_attn_prompt_kernel, n_full=n_full, tail=tail),
        grid=(nb, N_HEADS // ATT_HEADS, rows // ATT_TQ),
        in_specs=[pl.BlockSpec(memory_space=pltpu.SMEM), q_spec, kv_spec, kv_spec,
                  pl.BlockSpec((ATT_TK, ATT_TK), lambda b, g, i: (0, 0))],
        out_specs=q_spec,
        out_shape=jax.ShapeDtypeStruct((nb, rows, D_MODEL), BF16),
        scratch_shapes=[pltpu.VMEM((ATT_HEADS, ATT_TQ, LANES), BF16),
                        pltpu.VMEM((ATT_HEADS, ATT_TQ, ATT_TK), F32),
                        pltpu.VMEM((ATT_HEADS, ATT_TQ, ATT_TK), BF16),
                        pltpu.VMEM((ATT_HEADS, ATT_TQ, LANES), F32),
                        pltpu.VMEM((ATT_HEADS, ATT_TQ, LANES), F32)],
        compiler_params=pltpu.CompilerParams(dimension_semantics=("parallel", "parallel", "arbitrary"),
                                             vmem_limit_bytes=VMEM_LIMIT_BYTES),
        name="attn_prompt",
    )(bias, qb, ktb, vtb, tri)


def _attn_sample_kernel(pt_ref, qbd_ref, bias_ref, knew_ref, vnew_ref, *refs, n_new, npp):
    del pt_ref
    kpages, vpages = refs[:npp], refs[npp:2 * npp]
    tri_ref, selt_ref, fold_ref, o_ref, kcat, vcat, a_buf, acc_t, carry_ref = refs[2 * npp:]
    j = pl.program_id(1)
    nrow = N_HEADS * n_new
    chunk = ATT_TK
    n_chunks = npp * PAGE_SIZE // chunk
    qbd = qbd_ref[0]
    tri = tri_ref[...]

    @pl.when(j == 0)
    def _():
        qi = lax.broadcasted_iota(jnp.int32, (nrow, PAGE_SIZE), 0) // N_HEADS
        slot = lax.broadcasted_iota(jnp.int32, (nrow, PAGE_SIZE), 1)
        mask = slot < qi
        z = jnp.dot(qbd, knew_ref[0], preferred_element_type=F32) + bias_ref[:, 0:PAGE_SIZE]
        sp = jnp.where(mask, _softplus(z), 0.0)
        suffix = _suffix_sum(sp, tri[0:PAGE_SIZE, 0:PAGE_SIZE])
        a = jnp.where(mask, jnp.exp(z - suffix), 0.0).astype(BF16)
        acc_t[...] = lax.dot_general(vnew_ref[0], a, _NT_DIMS, preferred_element_type=F32)
        carry_ref[...] = jnp.broadcast_to(suffix[:, 0:1], carry_ref.shape)
        a_buf[...] = jnp.zeros_like(a_buf)

    for k in range(npp):
        vcat[:, k * PAGE_SIZE:(k + 1) * PAGE_SIZE] = vpages[k][0].astype(BF16)
    acc_t[...] += lax.dot_general(vcat[...], a_buf[...], _NT_DIMS, preferred_element_type=F32)

    for k in range(npp):
        kcat[:, k * PAGE_SIZE:(k + 1) * PAGE_SIZE] = kpages[k][0].astype(BF16)
    z = jnp.dot(qbd, kcat[...], preferred_element_type=F32) + bias_ref[...]
    sp = _softplus(z)
    carry = carry_ref[...]
    for c in reversed(range(n_chunks)):
        lanes = slice(c * chunk, (c + 1) * chunk)
        suffix = _suffix_sum(sp[:, lanes], tri) + jnp.concatenate([carry] * (chunk // LANES), axis=1)
        a_buf[:, lanes] = jnp.exp(z[:, lanes] - suffix).astype(BF16)
        carry = jnp.broadcast_to(suffix[:, 0:1], carry.shape)
    carry_ref[...] = carry

    @pl.when(j == pl.num_programs(1) - 1)
    def _():
        picked = (acc_t[...] * selt_ref[...]).astype(BF16)
        o = lax.dot_general(fold_ref[...], picked, _NT_DIMS, preferred_element_type=F32)
        o_ref[0] = o[0:n_new].astype(o_ref.dtype)


def _attn_sample_call(page_table, qbd, bias_rows, knew, vnew, cache_kt, cache_vt, tri, selt, fold, n_new):
    nseq, n_pages = page_table.shape
    npp = SAMPLE_PAGES
    nsteps = n_pages // npp
    nrow = N_HEADS * n_new

    def page_map(k, shift):
        def index_map(b, j, pt):
            step = jnp.clip(j - shift, 0, nsteps - 1)
            return (pt[b, n_pages - (step + 1) * npp + k], 0, 0)
        return index_map

    seq3 = lambda b, j, pt: (b, 0, 0)
    const2 = lambda b, j, pt: (0, 0)
    page_block = (1, D_MODEL, PAGE_SIZE)
    grid_spec = pltpu.PrefetchScalarGridSpec(
        num_scalar_prefetch=1,
        grid=(nseq, nsteps + 1),
        in_specs=([pl.BlockSpec((1, nrow, D_MODEL), seq3),
                   pl.BlockSpec((nrow, npp * PAGE_SIZE), const2),
                   pl.BlockSpec(page_block, seq3),
                   pl.BlockSpec(page_block, seq3)]
                  + [pl.BlockSpec(page_block, page_map(k, 0)) for k in range(npp)]
                  + [pl.BlockSpec(page_block, page_map(k, 1)) for k in range(npp)]
                  + [pl.BlockSpec((ATT_TK, ATT_TK), const2),
                     pl.BlockSpec((D_MODEL, nrow), const2),
                     pl.BlockSpec((HALO, nrow), const2)]),
        out_specs=pl.BlockSpec((1, n_new, D_MODEL), seq3),
        scratch_shapes=[pltpu.VMEM((D_MODEL, npp * PAGE_SIZE), BF16),
                        pltpu.VMEM((D_MODEL, npp * PAGE_SIZE), BF16),
                        pltpu.VMEM((nrow, npp * PAGE_SIZE), BF16),
                        pltpu.VMEM((D_MODEL, nrow), F32),
                        pltpu.VMEM((nrow, LANES), F32)])
    return pl.pallas_call(
        functools.partial(_attn_sample_kernel, n_new=n_new, npp=npp),
        grid_spec=grid_spec,
        out_shape=jax.ShapeDtypeStruct((nseq, n_new, D_MODEL), BF16),
        compiler_params=pltpu.CompilerParams(dimension_semantics=("parallel", "arbitrary"),
                                             vmem_limit_bytes=VMEM_LIMIT_BYTES),
        name="attn_sample",
    )(page_table, qbd, bias_rows, knew, vnew, *([cache_kt] * npp), *([cache_vt] * npp), tri, selt, fold)


def _suffix_matrix(n):
    j = jnp.arange(n)[:, None]
    s = jnp.arange(n)[None, :]
    return (j >= s).astype(BF16)


def kernel(x_prompt, x_sample, cache_k, cache_v, state_pool, page_table, meta_tokens, ln_g, ln_b, w_pool, b_pool,
           pool_scale, w_up, b_up, w_down, b_down, w_kv, w_q, w_o, sb_bias):
    nb, seq, _ = x_prompt.shape
    nseq, n_new, _ = x_sample.shape
    t_real = seq + N_META
    t_pad = -(-t_real // ATT_TK) * ATT_TK
    assert n_new <= HALO and page_table.shape[1] % SAMPLE_PAGES == 0

    wts0 = (w_pool[0].astype(BF16), b_pool[0:1], pool_scale[0:1], ln_g[0], ln_b[0],
            w_up[0].astype(BF16), b_up[0:1], w_down[0].astype(BF16), b_down[0:1],
            w_kv[:, :D_MODEL].T.astype(BF16), w_kv[:, D_MODEL:].T.astype(BF16), w_q[0].astype(BF16))
    wts1 = (w_o[0].astype(BF16), ln_g[1], ln_b[1], w_up[1].astype(BF16), b_up[1:2],
            w_down[1].astype(BF16), b_down[1:2])
    bias = sb_bias[0].astype(F32)
    tri = _suffix_matrix(ATT_TK)

    h_p, kt_p, vt_p, qb_p, ktb_p, vtb_p = _layer0_call(x_prompt, meta_tokens, t_real, wts0, prompt=True)
    o_p = _attn_prompt_call(bias, qb_p, ktb_p, vtb_p, tri, t_real)
    per_seq = seq // ROW_TILE
    start = lambda i: (i // per_seq) * t_pad + N_META + (i % per_seq) * ROW_TILE
    y_prompt = _layer1_call(o_p.reshape(nb * t_pad, D_MODEL), h_p.reshape(nb * t_pad, D_MODEL), wts1, ROW_TILE,
                            nb * per_seq, start).reshape(nb, seq, D_MODEL)
    pool_prompt = x_prompt[None, :, seq - POOL_CTX:, :]

    ctx = state_pool[0]
    u_s = jnp.concatenate([jnp.zeros((HALO - POOL_CTX, nseq, D_MODEL), F32), ctx.transpose(1, 0, 2),
                           x_sample.transpose(1, 0, 2)], axis=0)
    n_s = n_new * nseq
    h_s, kt_s, vt_s, qb_s, _, _ = _layer0_call(u_s.reshape(1, (HALO + n_new) * nseq, D_MODEL), meta_tokens, n_s,
                                               wts0, prompt=False, tm=n_s, stride=nseq)
    h_s = h_s.reshape(n_s, D_MODEL)
    qb_s = qb_s.reshape(n_new, nseq, D_MODEL).transpose(1, 0, 2)
    take_t = lambda a: a.reshape(D_MODEL, n_new, nseq).transpose(2, 0, 1)
    kt_s, vt_s = take_t(kt_s), take_t(vt_s)

    nrow = N_HEADS * n_new
    head_of_lane = jnp.arange(D_MODEL) // HEAD_DIM
    sel = head_of_lane[None, :] == (jnp.arange(nrow) % N_HEADS)[:, None]
    qbd = jnp.where(sel[None], jnp.repeat(qb_s, N_HEADS, axis=1), jnp.zeros((), BF16))
    bias_rows = jnp.broadcast_to(jnp.tile(bias, n_new)[:, None], (nrow, SAMPLE_PAGES * PAGE_SIZE))
    fold = (jnp.arange(HALO)[:, None] == (jnp.arange(nrow) // N_HEADS)[None, :]).astype(BF16)
    pad_new = lambda a: jnp.pad(a.astype(BF16), ((0, 0), (0, 0), (0, PAGE_SIZE - n_new)))
    n_phys = cache_k.shape[0]
    pages_t = lambda c: c.transpose(0, 2, 3, 1).reshape(n_phys, D_MODEL, PAGE_SIZE)
    o_s = _attn_sample_call(page_table, qbd, bias_rows, pad_new(kt_s), pad_new(vt_s),
                            pages_t(cache_k), pages_t(cache_v), tri, sel.T.astype(F32), fold, n_new)
    y_s = _layer1_call(o_s.transpose(1, 0, 2).reshape(n_s, D_MODEL), h_s, wts1, n_s)
    y_sample = y_s.reshape(n_new, nseq, D_MODEL).transpose(1, 0, 2)
    pool_sample = jnp.concatenate([ctx, x_sample], axis=1)[None, :, n_new:, :]

    rows = lambda a: a.reshape(a.shape[0], N_HEADS, HEAD_DIM, a.shape[2]).transpose(0, 3, 1, 2)
    return (y_prompt, y_sample, rows(kt_p), rows(vt_p), rows(kt_s), rows(vt_s), pool_prompt, pool_sample)
```
